```python
import jax, jax.numpy as jnp
from jax import lax
import numpy as np

D_MODEL = 1024
BATCH = 2
SEQ = 8192
DEPTH = 4
DEC_BATCH = 128
DEC_SEQ = 4
PAST_LEN = 8192
PAGE_SIZE = 128

N_MIXERS = 3
HEAD_DIM = 64
N_Q_HEADS = D_MODEL // HEAD_DIM
N_KV_HEADS = 4
GROUP = N_Q_HEADS // N_KV_HEADS
QKV_DIM = (N_Q_HEADS + 2 * N_KV_HEADS) * HEAD_DIM
Q_BLOCK = 128
CONV_WIDTH = 31
WINDOW = 128
ROPE_THETA = 500000.0
ROPE_DIM = HEAD_DIM // 4
D_FF = 4 * D_MODEL
RMS_EPS = 1e-6
LN_EPS = 1e-5
SB_BIAS_INIT = -7.0

SB_LAYERS = tuple(i for i in range(DEPTH) if i % N_MIXERS == 0)
CONV_LAYERS = tuple(i for i in range(DEPTH) if i % N_MIXERS == 1)
SWA_LAYERS = tuple(i for i in range(DEPTH) if i % N_MIXERS == 2)
N_SB = len(SB_LAYERS)
N_CONV = len(CONV_LAYERS)
N_SWA = len(SWA_LAYERS)

kernel_name = "hybrid_stickbreak_conformer_swa_decode_step"


def rms_norm(x, g):
    xf = x.astype(jnp.float32)
    y = xf * lax.rsqrt(jnp.mean(xf * xf, -1, keepdims=True) + RMS_EPS)
    return (y * g.astype(jnp.float32)).astype(x.dtype)


def layer_norm(x, g, b):
    xf = x.astype(jnp.float32)
    mu = jnp.mean(xf, -1, keepdims=True)
    var = jnp.mean(jnp.square(xf - mu), -1, keepdims=True)
    y = (xf - mu) * lax.rsqrt(var + LN_EPS) * g.astype(jnp.float32) + b.astype(jnp.float32)
    return y.astype(x.dtype)


def split_heads(qkv):
    B, T = qkv.shape[:2]
    q, k, v = jnp.split(qkv, [N_Q_HEADS * HEAD_DIM, (N_Q_HEADS + N_KV_HEADS) * HEAD_DIM], -1)
    return (q.reshape(B, T, N_KV_HEADS, GROUP, HEAD_DIM),
            k.reshape(B, T, N_KV_HEADS, HEAD_DIM),
            v.reshape(B, T, N_KV_HEADS, HEAD_DIM))


def partial_rope(x, pos):
    half = ROPE_DIM // 2
    inv = ROPE_THETA ** (-jnp.arange(half, dtype=jnp.float32) / half)
    ang = pos.astype(jnp.float32)[:, None] * inv[None, :]
    ang = ang.reshape((1, ang.shape[0]) + (1,) * (x.ndim - 3) + (half,))
    cos, sin = jnp.cos(ang), jnp.sin(ang)
    xr = x[..., :ROPE_DIM].astype(jnp.float32)
    x1, x2 = xr[..., :half], xr[..., half:]
    rot = jnp.concatenate([x1 * cos - x2 * sin, x2 * cos + x1 * sin], -1).astype(x.dtype)
    return jnp.concatenate([rot, x[..., ROPE_DIM:]], -1)


def stick_breaking(q, k, v, q_pos, k_pos, bias):
    z = jnp.einsum('btkgd,bskd->bkgts', q, k, preferred_element_type=jnp.float32) * (HEAD_DIM ** -0.5)
    z = z + bias.astype(jnp.float32)[:, :, None, None]
    mask = k_pos[None, :] < q_pos[:, None]
    sp = jnp.where(mask, jax.nn.softplus(z), 0.0)
    tail = lax.cumsum(sp, axis=z.ndim - 1, reverse=True)
    a = jnp.exp(jnp.where(mask, z - tail, -jnp.inf))
    return jnp.einsum('bkgts,bskd->btkgd', a.astype(v.dtype), v)


def sb_prompt(q, k, v, bias):
    B, S = q.shape[:2]
    nb = S // Q_BLOCK
    qb = q.reshape(B, nb, Q_BLOCK, N_KV_HEADS, GROUP, HEAD_DIM).swapaxes(0, 1)
    k_pos = jnp.arange(S, dtype=jnp.int32)

    def one_block(args):
        qi, bi = args
        q_pos = bi * Q_BLOCK + jnp.arange(Q_BLOCK, dtype=jnp.int32)
        return stick_breaking(qi, k, v, q_pos, k_pos, bias)

    out = lax.map(one_block, (qb, jnp.arange(nb, dtype=jnp.int32)))
    return out.swapaxes(0, 1).reshape(B, S, N_Q_HEADS * HEAD_DIM)


def sb_sample(q, k_new, v_new, k_past, v_past, bias):
    B, T = q.shape[:2]
    P = k_past.shape[1]
    k_all = jnp.concatenate([k_past, k_new], 1)
    v_all = jnp.concatenate([v_past, v_new], 1)
    k_pos = jnp.arange(P + T, dtype=jnp.int32)
    q_pos = P + jnp.arange(T, dtype=jnp.int32)
    return stick_breaking(q, k_all, v_all, q_pos, k_pos, bias).reshape(B, T, N_Q_HEADS * HEAD_DIM)


def conformer_conv(h, left, w_pw1, b_pw1, w_dw, b_dw, ln_g, ln_b, w_pw2, b_pw2):
    a, g = jnp.split(h @ w_pw1 + b_pw1, 2, -1)
    u = a * jax.nn.sigmoid(g)
    ext = jnp.concatenate([left, u], 1)
    c = lax.conv_general_dilated(ext, w_dw[:, None, :], window_strides=(1,), padding='VALID',
                                 dimension_numbers=('NWC', 'WIO', 'NWC'),
                                 feature_group_count=D_MODEL) + b_dw
    c = jax.nn.silu(layer_norm(c, ln_g, ln_b))
    y = c @ w_pw2 + b_pw2
    return y, ext[:, -(CONV_WIDTH - 1):]


def sink_softmax(z, mask, sinks):
    z = jnp.where(mask, z, -jnp.inf)
    s = sinks.astype(jnp.float32)[:, :, None, None]
    m = jnp.maximum(jnp.max(z, -1, keepdims=True), s)
    e = jnp.exp(z - m)
    return e / (jnp.sum(e, -1, keepdims=True) + jnp.exp(s - m))


def swa_prompt(q, k, v, sinks):
    B, S = q.shape[:2]
    nb = S // WINDOW
    qb = q.reshape(B, nb, WINDOW, N_KV_HEADS, GROUP, HEAD_DIM)
    kb = k.reshape(B, nb, WINDOW, N_KV_HEADS, HEAD_DIM)
    vb = v.reshape(B, nb, WINDOW, N_KV_HEADS, HEAD_DIM)
    pad = ((0, 0), (1, 0), (0, 0), (0, 0), (0, 0))
    k2 = jnp.concatenate([jnp.pad(kb, pad)[:, :-1], kb], 2)
    v2 = jnp.concatenate([jnp.pad(vb, pad)[:, :-1], vb], 2)
    qi = jnp.arange(WINDOW, dtype=jnp.int32)
    kj = jnp.arange(2 * WINDOW, dtype=jnp.int32) - WINDOW
    rel = qi[:, None] - kj[None, :]
    band = (rel >= 0) & (rel <= WINDOW)
    valid = (jnp.arange(nb, dtype=jnp.int32)[:, None, None] * WINDOW + kj[None, None, :]) >= 0
    mask = (band[None] & valid)[None, :, None, None]
    z = jnp.einsum('bnqkgd,bnskd->bnkgqs', qb, k2, preferred_element_type=jnp.float32) * (HEAD_DIM ** -0.5)
    p = sink_softmax(z, mask, sinks)
    o = jnp.einsum('bnkgqs,bnskd->bnqkgd', p.astype(v.dtype), v2)
    return o.reshape(B, S, N_Q_HEADS * HEAD_DIM)


def swa_sample(q, k_new, v_new, k_buf, v_buf, sinks):
    B, T = q.shape[:2]
    W = k_buf.shape[1]
    P = PAST_LEN
    k_all = jnp.concatenate([k_buf, k_new], 1)
    v_all = jnp.concatenate([v_buf, v_new], 1)
    k_pos = P - W + jnp.arange(W + T, dtype=jnp.int32)
    q_pos = P + jnp.arange(T, dtype=jnp.int32)
    rel = q_pos[:, None] - k_pos[None, :]
    mask = (rel >= 0) & (rel <= WINDOW)
    z = jnp.einsum('btkgd,bskd->bkgts', q, k_all, preferred_element_type=jnp.float32) * (HEAD_DIM ** -0.5)
    p = sink_softmax(z, mask, sinks)
    o = jnp.einsum('bkgts,bskd->btkgd', p.astype(v_all.dtype), v_all)
    return o.reshape(B, T, N_Q_HEADS * HEAD_DIM), k_all[:, -W:], v_all[:, -W:]


def sqrelu_mlp(h, w_up, w_down):
    u = jax.nn.relu(h @ w_up)
    return (u * u) @ w_down


def setup_inputs(seed: int = 0) -> dict:
    key = jax.random.key(seed)
    ks = jax.random.split(key, 32)

    def nrm(i, shape, scale):
        return jax.random.normal(ks[i], shape, jnp.float32) * scale

    n_pages = PAST_LEN // PAGE_SIZE
    n_used = DEC_BATCH * n_pages
    n_phys = n_used + max(1, n_used // 4)
    w_buf = min(WINDOW, PAST_LEN)
    page_table = jax.random.permutation(ks[31], n_phys)[:n_used].reshape(DEC_BATCH, n_pages).astype(jnp.int32)
    d_attn = N_Q_HEADS * HEAD_DIM
    return {
        "x_prompt": nrm(0, (BATCH, SEQ, D_MODEL), 1.0),
        "x_sample": nrm(1, (DEC_BATCH, DEC_SEQ, D_MODEL), 1.0),
        "cache_sb_k": nrm(2, (N_SB, n_phys, PAGE_SIZE, N_KV_HEADS, HEAD_DIM), 1.0),
        "cache_sb_v": nrm(3, (N_SB, n_phys, PAGE_SIZE, N_KV_HEADS, HEAD_DIM), 1.0),
        "state_conv": nrm(4, (N_CONV, DEC_BATCH, CONV_WIDTH - 1, D_MODEL), 0.5),
        "cache_swa_k": nrm(5, (N_SWA, DEC_BATCH, w_buf, N_KV_HEADS, HEAD_DIM), 1.0),
        "cache_swa_v": nrm(6, (N_SWA, DEC_BATCH, w_buf, N_KV_HEADS, HEAD_DIM), 1.0),
        "page_table": page_table,
        "norm_mix": 1.0 + nrm(7, (DEPTH, D_MODEL), 0.05),
        "norm_ffn": 1.0 + nrm(8, (DEPTH, D_MODEL), 0.05),
        "w_ffn_up": nrm(9, (DEPTH, D_MODEL, D_FF), D_MODEL ** -0.5),
        "w_ffn_down": nrm(10, (DEPTH, D_FF, D_MODEL), D_FF ** -0.5),
        "w_sb_qkv": nrm(11, (N_SB, D_MODEL, QKV_DIM), D_MODEL ** -0.5),
        "w_sb_o": nrm(12, (N_SB, d_attn, D_MODEL), d_attn ** -0.5),
        "sb_bias": SB_BIAS_INIT + nrm(27, (N_SB, N_Q_HEADS), 0.5),
        "w_cv_pw1": nrm(13, (N_CONV, D_MODEL, 2 * D_MODEL), D_MODEL ** -0.5),
        "b_cv_pw1": nrm(14, (N_CONV, 2 * D_MODEL), 0.01),
        "w_cv_dw": nrm(15, (N_CONV, CONV_WIDTH, D_MODEL), CONV_WIDTH ** -0.5),
        "b_cv_dw": nrm(16, (N_CONV, D_MODEL), 0.01),
        "cv_ln_g": 1.0 + nrm(17, (N_CONV, D_MODEL), 0.05),
        "cv_ln_b": nrm(18, (N_CONV, D_MODEL), 0.01),
        "w_cv_pw2": nrm(19, (N_CONV, D_MODEL, D_MODEL), D_MODEL ** -0.5),
        "b_cv_pw2": nrm(20, (N_CONV, D_MODEL), 0.01),
        "w_swa_qkv": nrm(21, (N_SWA, D_MODEL, QKV_DIM), D_MODEL ** -0.5),
        "b_swa_qkv": nrm(22, (N_SWA, QKV_DIM), 0.01),
        "swa_sinks": nrm(23, (N_SWA, N_Q_HEADS), 0.5),
        "w_swa_o": nrm(24, (N_SWA, d_attn, D_MODEL), d_attn ** -0.5),
        "b_swa_o": nrm(25, (N_SWA, D_MODEL), 0.01),
        "norm_final": 1.0 + nrm(26, (D_MODEL,), 0.05),
    }


def reference(x_prompt, x_sample, cache_sb_k, cache_sb_v, state_conv, cache_swa_k, cache_swa_v,
              page_table, norm_mix, norm_ffn, w_ffn_up, w_ffn_down, w_sb_qkv, w_sb_o, sb_bias,
              w_cv_pw1, b_cv_pw1, w_cv_dw, b_cv_dw, cv_ln_g, cv_ln_b, w_cv_pw2, b_cv_pw2,
              w_swa_qkv, b_swa_qkv, swa_sinks, w_swa_o, b_swa_o, norm_final):
    hp, hs = x_prompt, x_sample
    B, S = hp.shape[:2]
    DB, T = hs.shape[:2]
    pos_p = jnp.arange(S, dtype=jnp.int32)
    pos_s = PAST_LEN + jnp.arange(T, dtype=jnp.int32)
    sbkp, sbvp, sbks, sbvs = [], [], [], []
    cvp, cvs = [], []
    swkp, swvp, swks, swvs = [], [], [], []
    a_i = c_i = w_i = 0
    for layer in range(DEPTH):
        kind = layer % N_MIXERS
        np_ = rms_norm(hp, norm_mix[layer])
        ns_ = rms_norm(hs, norm_mix[layer])
        if kind == 0:
            bias = sb_bias[a_i].reshape(N_KV_HEADS, GROUP)
            qp, kp, vp = split_heads(np_ @ w_sb_qkv[a_i])
            qs, kn, vn = split_heads(ns_ @ w_sb_qkv[a_i])
            k_past = cache_sb_k[a_i][page_table].reshape(DB, -1, N_KV_HEADS, HEAD_DIM)
            v_past = cache_sb_v[a_i][page_table].reshape(DB, -1, N_KV_HEADS, HEAD_DIM)
            mp = sb_prompt(qp, kp, vp, bias) @ w_sb_o[a_i]
            ms = sb_sample(qs, kn, vn, k_past, v_past, bias) @ w_sb_o[a_i]
            sbkp.append(kp); sbvp.append(vp); sbks.append(kn); sbvs.append(vn)
            a_i += 1
        elif kind == 1:
            prm = (w_cv_pw1[c_i], b_cv_pw1[c_i], w_cv_dw[c_i], b_cv_dw[c_i],
                   cv_ln_g[c_i], cv_ln_b[c_i], w_cv_pw2[c_i], b_cv_pw2[c_i])
            zero_left = jnp.zeros((B, CONV_WIDTH - 1, D_MODEL), hp.dtype)
            mp, lp = conformer_conv(np_, zero_left, *prm)
            ms, ls = conformer_conv(ns_, state_conv[c_i], *prm)
            cvp.append(lp); cvs.append(ls)
            c_i += 1
        else:
            sinks = swa_sinks[w_i].reshape(N_KV_HEADS, GROUP)
            qp, kp, vp = split_heads(np_ @ w_swa_qkv[w_i] + b_swa_qkv[w_i])
            qs, kn, vn = split_heads(ns_ @ w_swa_qkv[w_i] + b_swa_qkv[w_i])
            qp, kp = partial_rope(qp, pos_p), partial_rope(kp, pos_p)
            qs, kn = partial_rope(qs, pos_s), partial_rope(kn, pos_s)
            w_buf = cache_swa_k.shape[2]
            mp = swa_prompt(qp, kp, vp, sinks) @ w_swa_o[w_i] + b_swa_o[w_i]
            os_, kbuf, vbuf = swa_sample(qs, kn, vn, cache_swa_k[w_i], cache_swa_v[w_i], sinks)
            ms = os_ @ w_swa_o[w_i] + b_swa_o[w_i]
            swkp.append(kp[:, -w_buf:]); swvp.append(vp[:, -w_buf:])
            swks.append(kbuf); swvs.append(vbuf)
            w_i += 1
        hp = hp + mp
        hs = hs + ms
        hp = hp + sqrelu_mlp(rms_norm(hp, norm_ffn[layer]), w_ffn_up[layer], w_ffn_down[layer])
        hs = hs + sqrelu_mlp(rms_norm(hs, norm_ffn[layer]), w_ffn_up[layer], w_ffn_down[layer])
    y_prompt = rms_norm(hp, norm_final)
    y_sample = rms_norm(hs, norm_final)
    return (y_prompt, y_sample,
            jnp.stack(sbkp), jnp.stack(sbvp), jnp.stack(sbks), jnp.stack(sbvs),
            jnp.stack(cvp), jnp.stack(cvs),
            jnp.stack(swkp), jnp.stack(swvp), jnp.stack(swks), jnp.stack(swvs))
```

```python
import functools

import jax
import jax.numpy as jnp
from jax import lax
from jax.experimental import pallas as pl
from jax.experimental.pallas import tpu as pltpu

F32 = jnp.float32
BF16 = jnp.bfloat16

HEAD_DIM = 64
N_Q_HEADS = 16
N_KV_HEADS = 4
GROUP = N_Q_HEADS // N_KV_HEADS
Q_DIM = N_Q_HEADS * HEAD_DIM
KV_DIM = N_KV_HEADS * HEAD_DIM
N_MIXERS = 3
CONV_WIDTH = 31
WINDOW = 128
ROPE_THETA = 500000.0
ROPE_DIM = HEAD_DIM // 4
RMS_EPS = 1e-6
LN_EPS = 1e-5
QK_SCALE = HEAD_DIM ** -0.5

LANES = 128
VMEM_LIMIT = 56 * 1024 * 1024

TOKEN_TILE = 512
FF_TILE = 1024
SB_Q_TILE = 128
SB_K_BLOCK = 256
SB_PAGES_PER_STEP = 8
CONV_TILE = 256
CONV_HALO = 32
CONV_SAMPLE_BATCH = 8


def _cparams(*sem):
    return pltpu.CompilerParams(dimension_semantics=sem, vmem_limit_bytes=VMEM_LIMIT)


def _rms(x, g):
    ms = jnp.mean(x * x, axis=-1, keepdims=True)
    return x * lax.rsqrt(ms + RMS_EPS) * g


def _softplus(z):
    return jnp.maximum(z, 0.0) + jnp.log(1.0 + jnp.exp(-jnp.abs(z)))


def _div_pow2(x, n):
    assert n & (n - 1) == 0
    return x >> (n.bit_length() - 1)


def _dot(a, b):
    return jnp.dot(a, b, preferred_element_type=F32)


def _dot_nt(a, b):
    return lax.dot_general(a, b, (((1,), (1,)), ((), ())), preferred_element_type=F32)


def _sb_qkv_epilogue(acc, extra, outs):
    q_ref, k_ref, v_ref = outs
    q_ref[...] = (acc[:, :Q_DIM] * QK_SCALE).astype(BF16)
    k_ref[...] = acc[:, Q_DIM:Q_DIM + KV_DIM]
    v_ref[...] = acc[:, Q_DIM + KV_DIM:]


def _rope_group(x, c, sa, sb):
    return x * c + pltpu.roll(x, LANES - ROPE_DIM // 2, 1) * sa + pltpu.roll(x, ROPE_DIM // 2, 1) * sb


def _swa_qkv_epilogue(acc, extra, outs):
    c_ref, sa_ref, sb_ref = extra
    q_ref, k_ref, v_ref = outs
    c, sa, sb = c_ref[...], sa_ref[...], sb_ref[...]
    for j in range(Q_DIM // LANES):
        x = acc[:, j * LANES:(j + 1) * LANES]
        q_ref[:, j * LANES:(j + 1) * LANES] = (_rope_group(x, c, sa, sb) * QK_SCALE).astype(BF16)
    for j in range(KV_DIM // LANES):
        x = acc[:, Q_DIM + j * LANES:Q_DIM + (j + 1) * LANES]
        k_ref[:, j * LANES:(j + 1) * LANES] = _rope_group(x, c, sa, sb)
    v_ref[...] = acc[:, Q_DIM + KV_DIM:]


def _glu_epilogue(acc, extra, outs):
    (u_ref,) = outs
    d = acc.shape[1] // 2
    u_ref[...] = acc[:, :d] * (1.0 / (1.0 + jnp.exp(-acc[:, d:])))


def _norm_linear_kernel(x_ref, g_ref, w_ref, b_ref, *refs, n_extra, epilogue):
    h = _rms(x_ref[...], g_ref[...]).astype(BF16)
    acc = _dot(h, w_ref[...]) + b_ref[...]
    epilogue(acc, refs[:n_extra], refs[n_extra:])


def _norm_linear(x, g, w, b, epilogue, out_cols, out_dtypes, extra=(), name="norm_linear"):
    n, d = x.shape
    n_out = w.shape[1]
    tm = TOKEN_TILE
    row = lambda i: (i, 0)
    fixed = lambda i: (0, 0)
    in_specs = [pl.BlockSpec((tm, d), row), pl.BlockSpec((1, d), fixed),
                pl.BlockSpec((d, n_out), fixed), pl.BlockSpec((1, n_out), fixed)]
    in_specs += [pl.BlockSpec((tm, e.shape[1]), row) for e in extra]
    return pl.pallas_call(
        functools.partial(_norm_linear_kernel, n_extra=len(extra), epilogue=epilogue),
        grid=(n // tm,),
        in_specs=in_specs,
        out_specs=[pl.BlockSpec((tm, c), row) for c in out_cols],
        out_shape=[jax.ShapeDtypeStruct((n, c), dt) for c, dt in zip(out_cols, out_dtypes)],
        compiler_params=_cparams("parallel"),
        name=name,
    )(x, g.reshape(1, d), w, b.reshape(1, n_out), *extra)


def _proj_mlp_kernel(o_ref, wo_ref, bo_ref, x_ref, g_ref, wu_ref, wd_ref, gf_ref, y_ref,
                     acc_ref, h_ref, *, final_norm):
    k = pl.program_id(1)

    @pl.when(k == 0)
    def _():
        x1 = x_ref[...] + _dot(o_ref[...], wo_ref[...]) + bo_ref[...]
        acc_ref[...] = x1
        h_ref[...] = _rms(x1, g_ref[...]).astype(BF16)

    u = jnp.maximum(_dot(h_ref[...], wu_ref[...]), 0.0)
    acc_ref[...] += _dot((u * u).astype(BF16), wd_ref[...])

    @pl.when(k == pl.num_programs(1) - 1)
    def _():
        y = acc_ref[...]
        y_ref[...] = _rms(y, gf_ref[...]) if final_norm else y


def _proj_mlp(o, wo, bo, x, g, wu, wd, gf, final_norm):
    n, d = x.shape
    d_ff = wu.shape[1]
    tm, tf = TOKEN_TILE, FF_TILE
    row = lambda i, k: (i, 0)
    fixed = lambda i, k: (0, 0)
    return pl.pallas_call(
        functools.partial(_proj_mlp_kernel, final_norm=final_norm),
        grid=(n // tm, d_ff // tf),
        in_specs=[pl.BlockSpec((tm, d), row), pl.BlockSpec((d, d), fixed), pl.BlockSpec((1, d), fixed),
                  pl.BlockSpec((tm, d), row), pl.BlockSpec((1, d), fixed),
                  pl.BlockSpec((d, tf), lambda i, k: (0, k)), pl.BlockSpec((tf, d), lambda i, k: (k, 0)),
                  pl.BlockSpec((1, d), fixed)],
        out_specs=pl.BlockSpec((tm, d), row),
        out_shape=jax.ShapeDtypeStruct((n, d), F32),
        scratch_shapes=[pltpu.VMEM((tm, d), F32), pltpu.VMEM((tm, d), BF16)],
        compiler_params=_cparams("parallel", "arbitrary"),
        name="proj_mlp",
    )(o, wo, bo.reshape(1, d), x, g.reshape(1, d), wu, wd, gf.reshape(1, d))


def _sb_block(z, lt, v, carry, acc, mask):
    sp = _softplus(z)
    if mask is not None:
        sp = jnp.where(mask, sp, 0.0)
    tail = _dot(sp.astype(BF16), lt)
    reps = z.shape[1] // LANES
    a = jnp.exp(z - tail - jnp.concatenate([carry] * reps, axis=1))
    if mask is not None:
        a = jnp.where(mask, a, 0.0)
    acc = acc + _dot(a.astype(BF16), v)
    carry = carry + jnp.broadcast_to(tail[:, 0:1], carry.shape)
    return carry, acc


def _sb_prompt_kernel(bias_ref, q_ref, kt_ref, v_ref, lt_ref, o_ref, *, tq, bk):
    kv = pl.program_id(1)
    i = pl.program_id(2)
    rows = GROUP * tq
    q = q_ref[...]
    q_all = jnp.concatenate([q[:, g * HEAD_DIM:(g + 1) * HEAD_DIM] for g in range(GROUP)], axis=0)
    lt = lt_ref[...]

    def logits(jb):
        z = _dot(q_all, kt_ref[jb, 0])
        return jnp.concatenate(
            [z[g * tq:(g + 1) * tq] + bias_ref[kv * GROUP + g] for g in range(GROUP)], axis=0)

    def values(jb):
        return v_ref[0, pl.ds(pl.multiple_of(jb * bk, bk), bk), :]

    jd = (i * tq) // bk
    kpos = jd * bk + lax.broadcasted_iota(jnp.int32, (rows, bk), 1)
    qpos = i * tq + (lax.broadcasted_iota(jnp.int32, (rows, bk), 0) & (tq - 1))
    carry = jnp.zeros((rows, LANES), F32)
    acc = jnp.zeros((rows, HEAD_DIM), F32)
    carry, acc = _sb_block(logits(jd), lt, values(jd), carry, acc, kpos < qpos)

    def body(n, c):
        jb = jd - 1 - n
        return _sb_block(logits(jb), lt, values(jb), c[0], c[1], None)

    carry, acc = lax.fori_loop(0, jd, body, (carry, acc))
    o_ref[...] = jnp.concatenate([acc[g * tq:(g + 1) * tq] for g in range(GROUP)], axis=1).astype(BF16)


def _sb_prompt(bias, q, kt4, v3, lt, batch, seq):
    tq, bk = SB_Q_TILE, SB_K_BLOCK
    nq, nb = seq // tq, seq // bk
    return pl.pallas_call(
        functools.partial(_sb_prompt_kernel, tq=tq, bk=bk),
        grid=(batch, N_KV_HEADS, nq),
        in_specs=[pl.BlockSpec(memory_space=pltpu.SMEM),
                  pl.BlockSpec((tq, GROUP * HEAD_DIM), lambda b, h, i: (b * nq + i, h)),
                  pl.BlockSpec((nb, 1, HEAD_DIM, bk), lambda b, h, i: (b, h, 0, 0)),
                  pl.BlockSpec((1, seq, HEAD_DIM), lambda b, h, i: (h, b, 0)),
                  pl.BlockSpec((bk, bk), lambda b, h, i: (0, 0))],
        out_specs=pl.BlockSpec((tq, GROUP * HEAD_DIM), lambda b, h, i: (b * nq + i, h)),
        out_shape=jax.ShapeDtypeStruct((batch * seq, Q_DIM), BF16),
        compiler_params=_cparams("parallel", "parallel", "arbitrary"),
        name="sb_prompt",
    )(bias, q, kt4, v3, lt)


def _block_diag_q(q):
    q4 = jnp.concatenate([q] * N_KV_HEADS, axis=1)
    r = lax.broadcasted_iota(jnp.int32, q4.shape, 0)
    c = lax.broadcasted_iota(jnp.int32, q4.shape, 1)
    rows_per_head = q.shape[0] // N_KV_HEADS
    return jnp.where(_div_pow2(r, rows_per_head) == _div_pow2(c, HEAD_DIM), q4, jnp.zeros_like(q4))


def _diag_blocks(o):
    rows_per_head = o.shape[0] // N_KV_HEADS
    return jnp.concatenate(
        [o[h * rows_per_head:(h + 1) * rows_per_head, h * HEAD_DIM:(h + 1) * HEAD_DIM]
         for h in range(N_KV_HEADS)], axis=0)


def _pad_rows(x, rows):
    return jnp.concatenate([x, jnp.zeros((rows - x.shape[0], x.shape[1]), x.dtype)], axis=0)


def _sb_sample_kernel(pt_ref, bias_ref, q_ref, kn_ref, vn_ref, lt_ref, *refs, npg, t_new):
    k_refs, v_refs = refs[:npg], refs[npg:2 * npg]
    o_ref, carry_ref, acc_ref = refs[2 * npg:]
    j = pl.program_id(1)
    bk = lt_ref.shape[0]
    qbd = _block_diag_q(q_ref[0])
    rows = qbd.shape[0]
    bias = jnp.concatenate([bias_ref[...]] * (bk // LANES), axis=1)
    lt = lt_ref[...]

    @pl.when(j == 0)
    def _():
        kn = _pad_rows(kn_ref[0], bk).astype(BF16)
        vn = _pad_rows(vn_ref[0], bk).astype(BF16)
        r = lax.broadcasted_iota(jnp.int32, (rows, bk), 0)
        c = lax.broadcasted_iota(jnp.int32, (rows, bk), 1)
        t = _div_pow2(r, GROUP) & (t_new - 1)
        carry, acc = _sb_block(_dot_nt(qbd, kn) + bias, lt, vn,
                               jnp.zeros((rows, LANES), F32), jnp.zeros((rows, KV_DIM), F32), c < t)
        carry_ref[...] = carry
        acc_ref[...] = acc

    carry, acc = carry_ref[...], acc_ref[...]
    pages_per_block = bk // k_refs[0].shape[2]
    for blk in reversed(range(npg // pages_per_block)):
        sel = range(blk * pages_per_block, (blk + 1) * pages_per_block)
        k = jnp.concatenate([k_refs[p][0, 0] for p in sel], axis=0).astype(BF16)
        v = jnp.concatenate([v_refs[p][0, 0] for p in sel], axis=0).astype(BF16)
        carry, acc = _sb_block(_dot_nt(qbd, k) + bias, lt, v, carry, acc, None)
    carry_ref[...] = carry
    acc_ref[...] = acc

    @pl.when(j == pl.num_programs(1) - 1)
    def _():
        o_ref[0] = _diag_blocks(acc).astype(BF16)


def _sb_sample(page_table, bias_rows, q_rows, k_new, v_new, lt, cache_k, cache_v, layer):
    db, n_pages = page_table.shape
    npg = SB_PAGES_PER_STEP
    nj = n_pages // npg
    page_size = cache_k.shape[2]
    rows = q_rows.shape[1]
    t_pad = k_new.shape[1]

    def page_spec(p):
        return pl.BlockSpec((1, 1, page_size, KV_DIM),
                            lambda b, j, pt: (layer, pt[b, (nj - 1 - j) * npg + p], 0, 0))

    per_b = lambda b, j, pt: (b, 0, 0)
    fixed = lambda b, j, pt: (0, 0)
    grid_spec = pltpu.PrefetchScalarGridSpec(
        num_scalar_prefetch=1,
        grid=(db, nj),
        in_specs=[pl.BlockSpec((rows, LANES), fixed),
                  pl.BlockSpec((1, rows, HEAD_DIM), per_b),
                  pl.BlockSpec((1, t_pad, KV_DIM), per_b),
                  pl.BlockSpec((1, t_pad, KV_DIM), per_b),
                  pl.BlockSpec(lt.shape, fixed)]
                 + [page_spec(p) for p in range(npg)] + [page_spec(p) for p in range(npg)],
        out_specs=pl.BlockSpec((1, rows, HEAD_DIM), per_b),
        scratch_shapes=[pltpu.VMEM((rows, LANES), F32), pltpu.VMEM((rows, KV_DIM), F32)],
    )
    return pl.pallas_call(
        functools.partial(_sb_sample_kernel, npg=npg, t_new=rows // N_Q_HEADS),
        grid_spec=grid_spec,
        out_shape=jax.ShapeDtypeStruct((db, rows, HEAD_DIM), BF16),
        compiler_params=_cparams("parallel", "arbitrary"),
        name="sb_sample",
    )(page_table, bias_rows, q_rows, k_new, v_new, lt, *([cache_k] * npg), *([cache_v] * npg))


def _ln_silu(c, g, b):
    mu = jnp.mean(c, axis=-1, keepdims=True)
    cc = c - mu
    var = jnp.mean(cc * cc, axis=-1, keepdims=True)
    y = cc * lax.rsqrt(var + LN_EPS) * g + b
    return y * (1.0 / (1.0 + jnp.exp(-y)))


def _conv_prompt_kernel(prev_ref, cur_ref, w_ref, b_ref, g_ref, be_ref, o_ref, ext_ref, *, tiles_per_seq):
    i = pl.program_id(0)
    tm = cur_ref.shape[0]
    halo = prev_ref.shape[0]
    first = (i % tiles_per_seq) == 0
    ext_ref[0:halo, :] = jnp.where(first, 0.0, prev_ref[...])
    ext_ref[halo:halo + tm, :] = cur_ref[...]
    base = halo - (CONV_WIDTH - 1)
    acc = jnp.zeros((tm, cur_ref.shape[1]), F32) + b_ref[...]
    for w in range(CONV_WIDTH):
        acc = acc + ext_ref[base + w:base + w + tm, :] * w_ref[w:w + 1, :]
    o_ref[...] = _ln_silu(acc, g_ref[...], be_ref[...]).astype(BF16)


def _conv_prompt(u, w_dw, b_dw, ln_g, ln_b, n_prompt, seq):
    d = u.shape[1]
    tm, halo = CONV_TILE, CONV_HALO
    ratio = tm // halo
    fixed = lambda i: (0, 0)
    vec = lambda a: a.reshape(1, d)
    return pl.pallas_call(
        functools.partial(_conv_prompt_kernel, tiles_per_seq=seq // tm),
        grid=(n_prompt // tm,),
        in_specs=[pl.BlockSpec((halo, d), lambda i: (jnp.maximum(i * ratio - 1, 0), 0)),
                  pl.BlockSpec((tm, d), lambda i: (i, 0)),
                  pl.BlockSpec((CONV_WIDTH, d), fixed),
                  pl.BlockSpec((1, d), fixed), pl.BlockSpec((1, d), fixed), pl.BlockSpec((1, d), fixed)],
        out_specs=pl.BlockSpec((tm, d), lambda i: (i, 0)),
        out_shape=jax.ShapeDtypeStruct((n_prompt, d), BF16),
        scratch_shapes=[pltpu.VMEM((halo + tm, d), F32)],
        compiler_params=_cparams("parallel"),
        name="conv_prompt",
    )(u, u, w_dw, vec(b_dw), vec(ln_g), vec(ln_b))


def _conv_sample_kernel(st_ref, u_ref, wst_ref, wu_ref, b_ref, g_ref, be_ref, o_ref):
    st, u = st_ref[...], u_ref[...]
    t_new = u.shape[1]
    for t in range(t_new):
        c = (jnp.sum(st * wst_ref[t][None], axis=1) + jnp.sum(u * wu_ref[t][None], axis=1) + b_ref[...])
        o_ref[t] = _ln_silu(c, g_ref[...], be_ref[...]).astype(BF16)


def _conv_sample(state, u_s, w_dw, b_dw, ln_g, ln_b):
    db, n_left, d = state.shape
    t_new = u_s.shape[1]
    bb = CONV_SAMPLE_BATCH
    r = jnp.arange(n_left)[None, :] - jnp.arange(t_new)[:, None]
    wst = jnp.where((r >= 0)[..., None], w_dw[jnp.clip(r, 0, CONV_WIDTH - 1)], 0.0)
    s = n_left - jnp.arange(t_new)[:, None] + jnp.arange(t_new)[None, :]
    wu = jnp.where((s <= CONV_WIDTH - 1)[..., None], w_dw[jnp.clip(s, 0, CONV_WIDTH - 1)], 0.0)
    fixed2 = lambda i: (0, 0)
    fixed3 = lambda i: (0, 0, 0)
    vec = lambda a: a.reshape(1, d)
    out = pl.pallas_call(
        _conv_sample_kernel,
        grid=(db // bb,),
        in_specs=[pl.BlockSpec((bb, n_left, d), lambda i: (i, 0, 0)),
                  pl.BlockSpec((bb, t_new, d), lambda i: (i, 0, 0)),
                  pl.BlockSpec((t_new, n_left, d), fixed3), pl.BlockSpec((t_new, t_new, d), fixed3),
                  pl.BlockSpec((1, d), fixed2), pl.BlockSpec((1, d), fixed2), pl.BlockSpec((1, d), fixed2)],
        out_specs=pl.BlockSpec((t_new, bb, d), lambda i: (0, i, 0)),
        out_shape=jax.ShapeDtypeStruct((t_new, db, d), BF16),
        compiler_params=_cparams("parallel"),
        name="conv_sample",
    )(state, u_s, wst, wu, vec(b_dw), vec(ln_g), vec(ln_b))
    return out.transpose(1, 0, 2).reshape(db * t_new, d)


def _swa_prompt_kernel(sink_ref, q_ref, kp_ref, kc_ref, vp_ref, vc_ref, o_ref):
    i = pl.program_id(1)
    w = q_ref.shape[0]
    q = q_ref[...]
    k2 = jnp.concatenate([kp_ref[...], kc_ref[...]], axis=0)
    v2 = jnp.concatenate([vp_ref[...], vc_ref[...]], axis=0)
    rows = GROUP * w
    r = lax.broadcasted_iota(jnp.int32, (rows, 2 * w), 0)
    c = lax.broadcasted_iota(jnp.int32, (rows, 2 * w), 1)
    rel = (r & (w - 1)) - c + w
    first_valid = jnp.where(i > 0, 0, w)
    mask = (rel >= 0) & (rel <= WINDOW) & (c >= first_valid)
    outs = []
    for h in range(N_KV_HEADS):
        qh = jnp.concatenate(
            [q[:, (h * GROUP + g) * HEAD_DIM:(h * GROUP + g + 1) * HEAD_DIM] for g in range(GROUP)], axis=0)
        z = _dot_nt(qh, k2[:, h * HEAD_DIM:(h + 1) * HEAD_DIM])
        z = jnp.where(mask, z, -jnp.inf)
        s = jnp.concatenate([jnp.full((w, 1), sink_ref[h * GROUP + g], F32) for g in range(GROUP)], axis=0)
        m = jnp.maximum(jnp.max(z, axis=-1, keepdims=True), s)
        e = jnp.exp(z - m)
        den = jnp.sum(e, axis=-1, keepdims=True) + jnp.exp(s - m)
        o = _dot(e.astype(BF16), v2[:, h * HEAD_DIM:(h + 1) * HEAD_DIM]) / den
        outs += [o[g * w:(g + 1) * w] for g in range(GROUP)]
    o_ref[...] = jnp.concatenate(outs, axis=1).astype(BF16)


def _swa_prompt(sinks, q, k, v, batch, seq):
    w = WINDOW
    nb = seq // w
    cur = lambda b, i: (b * nb + i, 0)
    prev = lambda b, i: (b * nb + jnp.maximum(i - 1, 0), 0)
    return pl.pallas_call(
        _swa_prompt_kernel,
        grid=(batch, nb),
        in_specs=[pl.BlockSpec(memory_space=pltpu.SMEM),
                  pl.BlockSpec((w, Q_DIM), cur),
                  pl.BlockSpec((w, KV_DIM), prev), pl.BlockSpec((w, KV_DIM), cur),
                  pl.BlockSpec((w, KV_DIM), prev), pl.BlockSpec((w, KV_DIM), cur)],
        out_specs=pl.BlockSpec((w, Q_DIM), cur),
        out_shape=jax.ShapeDtypeStruct((batch * seq, Q_DIM), BF16),
        compiler_params=_cparams("parallel", "parallel"),
        name="swa_prompt",
    )(sinks, q, k, k, v, v)


def _swa_sample_kernel(sink_ref, q_ref, kb_ref, vb_ref, kn_ref, vn_ref, o_ref, *, t_new):
    qbd = _block_diag_q(q_ref[0])
    rows = qbd.shape[0]
    wb = kb_ref.shape[1]
    kn = _pad_rows(kn_ref[0], wb).astype(BF16)
    vn = _pad_rows(vn_ref[0], wb).astype(BF16)
    r = lax.broadcasted_iota(jnp.int32, (rows, wb), 0)
    c = lax.broadcasted_iota(jnp.int32, (rows, wb), 1)
    t = _div_pow2(r, GROUP) & (t_new - 1)
    z1 = jnp.where((t + wb - c >= 0) & (t + wb - c <= WINDOW), _dot_nt(qbd, kb_ref[0].astype(BF16)), -jnp.inf)
    z2 = jnp.where((t - c >= 0) & (c < t_new), _dot_nt(qbd, kn), -jnp.inf)
    s = sink_ref[:, 0:1]
    m = jnp.maximum(jnp.maximum(jnp.max(z1, axis=-1, keepdims=True), jnp.max(z2, axis=-1, keepdims=True)), s)
    e1, e2 = jnp.exp(z1 - m), jnp.exp(z2 - m)
    den = jnp.sum(e1, axis=-1, keepdims=True) + jnp.sum(e2, axis=-1, keepdims=True) + jnp.exp(s - m)
    o = (_dot(e1.astype(BF16), vb_ref[0].astype(BF16)) + _dot(e2.astype(BF16), vn)) / den
    o_ref[0] = _diag_blocks(o).astype(BF16)


def _swa_sample(sink_rows, q_rows, k_buf, v_buf, k_new, v_new):
    db, wb, _ = k_buf.shape
    rows = q_rows.shape[1]
    t_pad = k_new.shape[1]
    per_b = lambda b: (b, 0, 0)
    return pl.pallas_call(
        functools.partial(_swa_sample_kernel, t_new=rows // N_Q_HEADS),
        grid=(db,),
        in_specs=[pl.BlockSpec((rows, LANES), lambda b: (0, 0)),
                  pl.BlockSpec((1, rows, HEAD_DIM), per_b),
                  pl.BlockSpec((1, wb, KV_DIM), per_b), pl.BlockSpec((1, wb, KV_DIM), per_b),
                  pl.BlockSpec((1, t_pad, KV_DIM), per_b), pl.BlockSpec((1, t_pad, KV_DIM), per_b)],
        out_specs=pl.BlockSpec((1, rows, HEAD_DIM), per_b),
        out_shape=jax.ShapeDtypeStruct((db, rows, HEAD_DIM), BF16),
        compiler_params=_cparams("parallel"),
        name="swa_sample",
    )(sink_rows, q_rows, k_buf, v_buf, k_new, v_new)


def _sample_q_rows(q_s, db, t):
    return q_s.reshape(db, t, N_KV_HEADS, GROUP, HEAD_DIM).transpose(0, 2, 1, 3, 4).reshape(
        db, N_Q_HEADS * t, HEAD_DIM)


def _sample_o_tokens(o_rows, db, t):
    return o_rows.reshape(db, N_KV_HEADS, t, GROUP, HEAD_DIM).transpose(0, 2, 1, 3, 4).reshape(db * t, Q_DIM)


def _head_rows(per_head, t):
    rows = jnp.broadcast_to(per_head.reshape(N_KV_HEADS, 1, GROUP), (N_KV_HEADS, t, GROUP)).reshape(-1, 1)
    return jnp.broadcast_to(rows, (rows.shape[0], LANES)).astype(F32)


def _pad_new(x, db, t):
    return jnp.pad(x.reshape(db, t, KV_DIM), ((0, 0), (0, 8 - t), (0, 0)))


def _rope_tables(pos):
    half = ROPE_DIM // 2
    inv = ROPE_THETA ** (-jnp.arange(half, dtype=F32) / half)
    ang = pos.astype(F32)[:, None] * inv[None, :]
    cos, sin = jnp.cos(ang), jnp.sin(ang)
    n = pos.shape[0]
    rest = HEAD_DIM - ROPE_DIM
    c = jnp.concatenate([cos, cos, jnp.ones((n, rest), F32)], axis=1)
    sa = jnp.concatenate([-sin, jnp.zeros((n, half + rest), F32)], axis=1)
    sb = jnp.concatenate([jnp.zeros((n, half), F32), sin, jnp.zeros((n, rest), F32)], axis=1)
    rep = LANES // HEAD_DIM
    return tuple(jnp.concatenate([x] * rep, axis=1) for x in (c, sa, sb))


def kernel(x_prompt, x_sample, cache_sb_k, cache_sb_v, state_conv, cache_swa_k, cache_swa_v, page_table,
           norm_mix, norm_ffn, w_ffn_up, w_ffn_down, w_sb_qkv, w_sb_o, sb_bias,
           w_cv_pw1, b_cv_pw1, w_cv_dw, b_cv_dw, cv_ln_g, cv_ln_b, w_cv_pw2, b_cv_pw2,
           w_swa_qkv, b_swa_qkv, swa_sinks, w_swa_o, b_swa_o, norm_final):
    batch, seq, d = x_prompt.shape
    db, t_new, _ = x_sample.shape
    depth = norm_mix.shape[0]
    past_len = page_table.shape[1] * cache_sb_k.shape[2]
    n_p, n_s = batch * seq, db * t_new
    assert (n_p + n_s) % TOKEN_TILE == 0 and n_p % TOKEN_TILE == 0
    assert cache_swa_k.shape[2] == WINDOW and t_new <= 8

    x = jnp.concatenate([x_prompt.reshape(n_p, d), x_sample.reshape(n_s, d)], axis=0)
    n_phys, page = cache_sb_k.shape[1], cache_sb_k.shape[2]
    cache_k = cache_sb_k.reshape(cache_sb_k.shape[0], n_phys, page, KV_DIM)
    cache_v = cache_sb_v.reshape(cache_sb_v.shape[0], n_phys, page, KV_DIM)
    jj = jnp.arange(SB_K_BLOCK)
    lt = (jj[:, None] >= jj[None, :]).astype(BF16)
    zero_bias_d = jnp.zeros((d,), F32)
    nb = seq // SB_K_BLOCK

    outs = {k: [] for k in ("sbkp", "sbvp", "sbks", "sbvs", "cvp", "cvs", "swkp", "swvp", "swks", "swvs")}
    a_i = c_i = w_i = 0
    for layer in range(depth):
        kind = layer % N_MIXERS
        if kind == 0:
            q, k, v = _norm_linear(
                x, norm_mix[layer], w_sb_qkv[a_i].astype(BF16), jnp.zeros((Q_DIM + 2 * KV_DIM,), F32),
                _sb_qkv_epilogue, (Q_DIM, KV_DIM, KV_DIM), (BF16, F32, F32), name="sb_qkv")
            kp, vp = k[:n_p], v[:n_p]
            kt4 = kp.astype(BF16).reshape(batch * nb, SB_K_BLOCK, N_KV_HEADS, HEAD_DIM).transpose(0, 2, 3, 1)
            v3 = vp.astype(BF16).reshape(n_p, N_KV_HEADS, HEAD_DIM).transpose(1, 0, 2)
            o_p = _sb_prompt(sb_bias[a_i], q, kt4, v3, lt, batch, seq)
            o_s = _sb_sample(page_table, _head_rows(sb_bias[a_i], t_new), _sample_q_rows(q[n_p:], db, t_new),
                             _pad_new(k[n_p:], db, t_new), _pad_new(v[n_p:], db, t_new), lt,
                             cache_k, cache_v, a_i)
            o = jnp.concatenate([o_p, _sample_o_tokens(o_s, db, t_new)], axis=0)
            w_o, b_o = w_sb_o[a_i], zero_bias_d
            outs["sbkp"].append(kp.reshape(batch, seq, N_KV_HEADS, HEAD_DIM))
            outs["sbvp"].append(vp.reshape(batch, seq, N_KV_HEADS, HEAD_DIM))
            outs["sbks"].append(k[n_p:].reshape(db, t_new, N_KV_HEADS, HEAD_DIM))
            outs["sbvs"].append(v[n_p:].reshape(db, t_new, N_KV_HEADS, HEAD_DIM))
            a_i += 1
        elif kind == 1:
            (u,) = _norm_linear(x, norm_mix[layer], w_cv_pw1[c_i].astype(BF16), b_cv_pw1[c_i],
                                _glu_epilogue, (d,), (F32,), name="conv_pw1")
            u_p, u_s = u[:n_p].reshape(batch, seq, d), u[n_p:].reshape(db, t_new, d)
            c_p = _conv_prompt(u, w_cv_dw[c_i], b_cv_dw[c_i], cv_ln_g[c_i], cv_ln_b[c_i], n_p, seq)
            c_s = _conv_sample(state_conv[c_i], u_s, w_cv_dw[c_i], b_cv_dw[c_i], cv_ln_g[c_i], cv_ln_b[c_i])
            o = jnp.concatenate([c_p, c_s], axis=0)
            w_o, b_o = w_cv_pw2[c_i], b_cv_pw2[c_i]
            n_left = CONV_WIDTH - 1
            outs["cvp"].append(u_p[:, seq - n_left:])
            outs["cvs"].append(jnp.concatenate([state_conv[c_i], u_s], axis=1)[:, -n_left:])
            c_i += 1
        else:
            pos = jnp.concatenate([jnp.tile(jnp.arange(seq, dtype=jnp.int32), batch),
                                   jnp.tile(past_len + jnp.arange(t_new, dtype=jnp.int32), db)])
            q, k, v = _norm_linear(
                x, norm_mix[layer], w_swa_qkv[w_i].astype(BF16), b_swa_qkv[w_i],
                _swa_qkv_epilogue, (Q_DIM, KV_DIM, KV_DIM), (BF16, F32, F32),
                extra=_rope_tables(pos), name="swa_qkv")
            kp, vp = k[:n_p], v[:n_p]
            o_p = _swa_prompt(swa_sinks[w_i], q, kp.astype(BF16), vp.astype(BF16), batch, seq)
            k_new, v_new = k[n_p:].reshape(db, t_new, KV_DIM), v[n_p:].reshape(db, t_new, KV_DIM)
            k_buf = cache_swa_k[w_i].reshape(db, WINDOW, KV_DIM)
            v_buf = cache_swa_v[w_i].reshape(db, WINDOW, KV_DIM)
            o_s = _swa_sample(_head_rows(swa_sinks[w_i], t_new), _sample_q_rows(q[n_p:], db, t_new),
                              k_buf, v_buf, _pad_new(k[n_p:], db, t_new), _pad_new(v[n_p:], db, t_new))
            o = jnp.concatenate([o_p, _sample_o_tokens(o_s, db, t_new)], axis=0)
            w_o, b_o = w_swa_o[w_i], b_swa_o[w_i]
            heads = (N_KV_HEADS, HEAD_DIM)
            outs["swkp"].append(kp.reshape(batch, seq, *heads)[:, seq - WINDOW:])
            outs["swvp"].append(vp.reshape(batch, seq, *heads)[:, seq - WINDOW:])
            outs["swks"].append(jnp.concatenate([k_buf, k_new], axis=1)[:, -WINDOW:].reshape(db, WINDOW, *heads))
            outs["swvs"].append(jnp.concatenate([v_buf, v_new], axis=1)[:, -WINDOW:].reshape(db, WINDOW, *heads))
            w_i += 1
        x = _proj_mlp(o, w_o.astype(BF16), b_o, x, norm_ffn[layer], w_ffn_up[layer].astype(BF16),
                      w_ffn_down[layer].astype(BF16), norm_final, final_norm=(layer == depth - 1))

    y_prompt = x[:n_p].reshape(batch, seq, d)
    y_sample = x[n_p:].reshape(db, t_new, d)
    return (y_prompt, y_sample,
            jnp.stack(outs["sbkp"]), jnp.stack(outs["sbvp"]), jnp.stack(outs["sbks"]), jnp.stack(outs["sbvs"]),
            jnp.stack(outs["cvp"]), jnp.stack(outs["cvs"]),
            jnp.stack(outs["swkp"]), jnp.stack(outs["swvp"]), jnp.stack(outs["swks"]), jnp.stack(outs["swvs"]))
```

```python
import functools

import jax
import jax.numpy as jnp
from jax import lax
from jax.experimental import pallas as pl
from jax.experimental.pallas import tpu as pltpu

F32 = jnp.float32
BF16 = jnp.bfloat16

HEAD_DIM = 64
N_Q_HEADS = 16
N_KV_HEADS = 4
GROUP = N_Q_HEADS // N_KV_HEADS
Q_DIM = N_Q_HEADS * HEAD_DIM
KV_DIM = N_KV_HEADS * HEAD_DIM
N_MIXERS = 3
CONV_WIDTH = 31
WINDOW = 128
ROPE_THETA = 500000.0
ROPE_DIM = HEAD_DIM // 4
RMS_EPS = 1e-6
LN_EPS = 1e-5
QK_SCALE = HEAD_DIM ** -0.5
LOG2E = 1.4426950408889634
MASKED_LOGIT = -1e30

LANES = 128
VMEM_LIMIT = 56 * 1024 * 1024

TOKEN_TILE = 512
FF_TILE = 1024
SB_Q_TILE = 128
SB_K_BLOCK = 256
SB_PAGES_PER_STEP = 16
CONV_TILE = 256
CONV_HALO = 32
CONV_SAMPLE_BATCH = 8


def _cparams(*sem):
    return pltpu.CompilerParams(dimension_semantics=sem, vmem_limit_bytes=VMEM_LIMIT)


def _rms(x, g):
    ms = jnp.mean(x * x, axis=-1, keepdims=True)
    return x * lax.rsqrt(ms + RMS_EPS) * g


def _softplus2(y):
    return jnp.maximum(y, 0.0) + jnp.log(1.0 + jnp.exp2(-jnp.abs(y))) * LOG2E


def _div_pow2(x, n):
    assert n & (n - 1) == 0
    return x >> (n.bit_length() - 1)


def _dot(a, b):
    return jnp.dot(a, b, preferred_element_type=F32)


def _dot_nt(a, b):
    return lax.dot_general(a, b, (((1,), (1,)), ((), ())), preferred_element_type=F32)


def _sb_qkv_epilogue(acc, extra, outs):
    q_ref, k_ref, v_ref = outs
    q_ref[...] = (acc[:, :Q_DIM] * (QK_SCALE * LOG2E)).astype(BF16)
    k_ref[...] = acc[:, Q_DIM:Q_DIM + KV_DIM]
    v_ref[...] = acc[:, Q_DIM + KV_DIM:]


def _sb_qkv_prompt_epilogue(acc, extra, outs):
    q_ref, ktf_ref, vtf_ref, ktb_ref, vtb_ref = outs
    q_ref[...] = (acc[:, :Q_DIM] * (QK_SCALE * LOG2E)).astype(BF16)
    bk = ktb_ref.shape[2]
    for lo, f_ref, b_ref in ((Q_DIM, ktf_ref, ktb_ref), (Q_DIM + KV_DIM, vtf_ref, vtb_ref)):
        t = acc[:, lo:lo + KV_DIM].T
        f_ref[0] = t
        for blk in range(b_ref.shape[0]):
            b_ref[blk] = t[:, blk * bk:(blk + 1) * bk].astype(BF16)


def _rope_group(x, c, sa, sb):
    return x * c + pltpu.roll(x, LANES - ROPE_DIM // 2, 1) * sa + pltpu.roll(x, ROPE_DIM // 2, 1) * sb


def _swa_qkv_epilogue(acc, extra, outs):
    c_ref, sa_ref, sb_ref = extra
    q_ref, k_ref, v_ref = outs
    c, sa, sb = c_ref[...], sa_ref[...], sb_ref[...]
    for j in range(Q_DIM // LANES):
        x = acc[:, j * LANES:(j + 1) * LANES]
        q_ref[:, j * LANES:(j + 1) * LANES] = (_rope_group(x, c, sa, sb) * QK_SCALE).astype(BF16)
    for j in range(KV_DIM // LANES):
        x = acc[:, Q_DIM + j * LANES:Q_DIM + (j + 1) * LANES]
        k_ref[:, j * LANES:(j + 1) * LANES] = _rope_group(x, c, sa, sb)
    v_ref[...] = acc[:, Q_DIM + KV_DIM:]


def _glu_epilogue(acc, extra, outs):
    (u_ref,) = outs
    d = acc.shape[1] // 2
    u_ref[...] = acc[:, :d] * (1.0 / (1.0 + jnp.exp(-acc[:, d:])))


def _norm_linear_kernel(x_ref, g_ref, w_ref, b_ref, *refs, n_extra, epilogue):
    h = _rms(x_ref[...], g_ref[...]).astype(BF16)
    acc = _dot(h, w_ref[...]) + b_ref[...]
    epilogue(acc, refs[:n_extra], refs[n_extra:])


def _row_out(n, cols, dtype):
    return (jax.ShapeDtypeStruct((n, cols), dtype), pl.BlockSpec((TOKEN_TILE, cols), lambda i: (i, 0)))


def _norm_linear(x, g, w, b, epilogue, outs, extra=(), first_tile=0, n_tiles=None, name="norm_linear"):
    n, d = x.shape
    n_out = w.shape[1]
    tm = TOKEN_TILE
    n_tiles = n // tm - first_tile if n_tiles is None else n_tiles
    row = lambda i: (i + first_tile, 0)
    fixed = lambda i: (0, 0)
    in_specs = [pl.BlockSpec((tm, d), row), pl.BlockSpec((1, d), fixed),
                pl.BlockSpec((d, n_out), fixed), pl.BlockSpec((1, n_out), fixed)]
    in_specs += [pl.BlockSpec((tm, e.shape[1]), row) for e in extra]
    return pl.pallas_call(
        functools.partial(_norm_linear_kernel, n_extra=len(extra), epilogue=epilogue),
        grid=(n_tiles,),
        in_specs=in_specs,
        out_specs=[spec for _, spec in outs],
        out_shape=[shape for shape, _ in outs],
        compiler_params=_cparams("parallel"),
        name=name,
    )(x, g.reshape(1, d), w, b.reshape(1, n_out), *extra)


def _proj_mlp_kernel(o_ref, wo_ref, bo_ref, x_ref, g_ref, wu_ref, wd_ref, gf_ref, y_ref,
                     acc_ref, h_ref, *, final_norm):
    k = pl.program_id(1)

    @pl.when(k == 0)
    def _():
        x1 = x_ref[...] + _dot(o_ref[...], wo_ref[...]) + bo_ref[...]
        acc_ref[...] = x1
        h_ref[...] = _rms(x1, g_ref[...]).astype(BF16)

    u = jnp.maximum(_dot(h_ref[...], wu_ref[...]), 0.0)
    acc_ref[...] += _dot((u * u).astype(BF16), wd_ref[...])

    @pl.when(k == pl.num_programs(1) - 1)
    def _():
        y = acc_ref[...]
        y_ref[...] = _rms(y, gf_ref[...]) if final_norm else y


def _proj_mlp(o, wo, bo, x, g, wu, wd, gf, final_norm):
    n, d = x.shape
    d_ff = wu.shape[1]
    tm, tf = TOKEN_TILE, FF_TILE
    row = lambda i, k: (i, 0)
    fixed = lambda i, k: (0, 0)
    return pl.pallas_call(
        functools.partial(_proj_mlp_kernel, final_norm=final_norm),
        grid=(n // tm, d_ff // tf),
        in_specs=[pl.BlockSpec((tm, d), row), pl.BlockSpec((d, d), fixed), pl.BlockSpec((1, d), fixed),
                  pl.BlockSpec((tm, d), row), pl.BlockSpec((1, d), fixed),
                  pl.BlockSpec((d, tf), lambda i, k: (0, k)), pl.BlockSpec((tf, d), lambda i, k: (k, 0)),
                  pl.BlockSpec((1, d), fixed)],
        out_specs=pl.BlockSpec((tm, d), row),
        out_shape=jax.ShapeDtypeStruct((n, d), F32),
        scratch_shapes=[pltpu.VMEM((tm, d), F32), pltpu.VMEM((tm, d), BF16)],
        compiler_params=_cparams("parallel", "arbitrary"),
        name="proj_mlp",
    )(o, wo, bo.reshape(1, d), x, g.reshape(1, d), wu, wd, gf.reshape(1, d))


def _sb_prompt_kernel(bias_ref, q_ref, kt_ref, vt_ref, lt_ref, o_ref,
                      y_scr, sp_scr, a_scr, carry_scr, acc_scr, *, tq, bk):
    kv = pl.program_id(1)
    i = pl.program_id(2)
    rows = GROUP * tq
    q = q_ref[...]
    q_all = jnp.concatenate([q[:, g * HEAD_DIM:(g + 1) * HEAD_DIM] for g in range(GROUP)], axis=0)
    jd = (i * tq) // bk

    def logits_mm(jb):
        return _dot(q_all, kt_ref[jb])

    def logits_post(y, jb, diagonal):
        y = jnp.concatenate(
            [y[g * tq:(g + 1) * tq] + bias_ref[kv * GROUP + g] for g in range(GROUP)], axis=0)
        sp = _softplus2(y)
        if diagonal:
            kpos = jb * bk + lax.broadcasted_iota(jnp.int32, (rows, bk), 1)
            qpos = i * tq + (lax.broadcasted_iota(jnp.int32, (rows, bk), 0) & (tq - 1))
            sp = jnp.where(kpos < qpos, sp, 0.0)
            y = jnp.where(kpos < qpos, y, MASKED_LOGIT)
        y_scr[...] = y
        sp_scr[...] = sp.astype(BF16)

    def weights_mm():
        return _dot(sp_scr[...], lt_ref[...])

    def weights_post(tail):
        c = carry_scr[...]
        a_scr[...] = jnp.exp2(y_scr[...] - tail - jnp.concatenate([c] * (bk // LANES), axis=1)).astype(BF16)
        carry_scr[...] = c + jnp.broadcast_to(tail[:, 0:1], c.shape)

    def values_mm(jb):
        return _dot_nt(a_scr[...], vt_ref[jb])

    def values_post(av):
        acc_scr[...] += av

    carry_scr[...] = jnp.zeros_like(carry_scr)
    acc_scr[...] = jnp.zeros_like(acc_scr)
    logits_post(logits_mm(jd), jd, True)

    @pl.when(jd == 0)
    def _():
        weights_post(weights_mm())

    @pl.when(jd >= 1)
    def _():
        y = logits_mm(jd - 1)
        tail = weights_mm()
        weights_post(tail)
        logits_post(y, jd - 1, False)

    def body(n, _):
        y = logits_mm(jd - n)
        tail = weights_mm()
        av = values_mm(jd - n + 2)
        weights_post(tail)
        logits_post(y, jd - n, False)
        values_post(av)
        return 0

    lax.fori_loop(2, jd + 1, body, 0)

    @pl.when(jd >= 1)
    def _():
        tail = weights_mm()
        av = values_mm(1)
        weights_post(tail)
        values_post(av)

    values_post(values_mm(0))
    acc = acc_scr[...]
    o_ref[...] = jnp.concatenate([acc[g * tq:(g + 1) * tq] for g in range(GROUP)], axis=1).astype(BF16)


def _sb_prompt(bias2, q, ktb, vtb, lt, batch, seq):
    tq, bk = SB_Q_TILE, SB_K_BLOCK
    nq, nb = seq // tq, seq // bk
    rows = GROUP * tq
    kv_spec = pl.BlockSpec((nb, HEAD_DIM, bk), lambda b, h, i: (b, h, 0))
    return pl.pallas_call(
        functools.partial(_sb_prompt_kernel, tq=tq, bk=bk),
        grid=(batch, N_KV_HEADS, nq),
        in_specs=[pl.BlockSpec(memory_space=pltpu.SMEM),
                  pl.BlockSpec((tq, GROUP * HEAD_DIM), lambda b, h, i: (b * nq + i, h)),
                  kv_spec, kv_spec,
                  pl.BlockSpec((bk, bk), lambda b, h, i: (0, 0))],
        out_specs=pl.BlockSpec((tq, GROUP * HEAD_DIM), lambda b, h, i: (b * nq + i, h)),
        out_shape=jax.ShapeDtypeStruct((batch * seq, Q_DIM), BF16),
        scratch_shapes=[pltpu.VMEM((rows, bk), F32), pltpu.VMEM((rows, bk), BF16), pltpu.VMEM((rows, bk), BF16),
                        pltpu.VMEM((rows, LANES), F32), pltpu.VMEM((rows, HEAD_DIM), F32)],
        compiler_params=_cparams("parallel", "parallel", "arbitrary"),
        name="sb_prompt",
    )(bias2, q, ktb, vtb, lt)


def _block_diag_q(q):
    q4 = jnp.concatenate([q] * N_KV_HEADS, axis=1)
    r = lax.broadcasted_iota(jnp.int32, q4.shape, 0)
    c = lax.broadcasted_iota(jnp.int32, q4.shape, 1)
    rows_per_head = q.shape[0] // N_KV_HEADS
    return jnp.where(_div_pow2(r, rows_per_head) == _div_pow2(c, HEAD_DIM), q4, jnp.zeros_like(q4))


def _diag_blocks(o):
    rows_per_head = o.shape[0] // N_KV_HEADS
    return jnp.concatenate(
        [o[h * rows_per_head:(h + 1) * rows_per_head, h * HEAD_DIM:(h + 1) * HEAD_DIM]
         for h in range(N_KV_HEADS)], axis=0)


def _pad_rows(x, rows):
    return jnp.concatenate([x, jnp.zeros((rows - x.shape[0], x.shape[1]), x.dtype)], axis=0)


def _sb_sample_block(qbd, bias, kt, vt, lt, carry, acc_t, mask):
    y = _dot(qbd, kt) + bias
    sp = _softplus2(y)
    if mask is not None:
        sp = jnp.where(mask, sp, 0.0)
    tail = _dot(sp.astype(BF16), lt)
    a = jnp.exp2(y - tail - jnp.concatenate([carry] * (y.shape[1] // LANES), axis=1))
    if mask is not None:
        a = jnp.where(mask, a, 0.0)
    acc_t = acc_t + _dot_nt(vt, a.astype(BF16))
    carry = carry + jnp.broadcast_to(tail[:, 0:1], carry.shape)
    return carry, acc_t


def _sb_sample_kernel(pt_ref, bias_ref, q_ref, kn_ref, vn_ref, lt_ref, *refs, npg, t_new):
    k_refs, v_refs = refs[:npg], refs[npg:2 * npg]
    o_ref, carry_ref, acc_ref = refs[2 * npg:]
    j = pl.program_id(1)
    bk = lt_ref.shape[0]
    page = k_refs[0].shape[3]
    qbd = _block_diag_q(q_ref[0])
    rows = qbd.shape[0]
    bias1 = bias_ref[...]
    bias = jnp.concatenate([bias1] * (bk // LANES), axis=1)
    lt = lt_ref[...]

    @pl.when(j == 0)
    def _():
        r = lax.broadcasted_iota(jnp.int32, (rows, page), 0)
        c = lax.broadcasted_iota(jnp.int32, (rows, page), 1)
        t = _div_pow2(r, GROUP) & (t_new - 1)
        carry, acc_t = _sb_sample_block(
            qbd, bias1, kn_ref[0], vn_ref[0], lt[:page, :page],
            jnp.zeros((rows, LANES), F32), jnp.zeros((KV_DIM, rows), F32), c < t)
        carry_ref[...] = carry
        acc_ref[...] = acc_t

    carry, acc_t = carry_ref[...], acc_ref[...]
    pages_per_block = bk // page
    blocks = [range(b * pages_per_block, (b + 1) * pages_per_block)
              for b in reversed(range(npg // pages_per_block))]
    cat = lambda refs, sel: jnp.concatenate([refs[p][0, 0] for p in sel], axis=1).astype(BF16)
    ys = [_dot(qbd, cat(k_refs, sel)) + bias for sel in blocks]
    tails = [_dot(_softplus2(y).astype(BF16), lt) for y in ys]
    for y, tail, sel in zip(ys, tails, blocks):
        a = jnp.exp2(y - tail - jnp.concatenate([carry] * (bk // LANES), axis=1))
        acc_t = acc_t + _dot_nt(cat(v_refs, sel), a.astype(BF16))
        carry = carry + jnp.broadcast_to(tail[:, 0:1], carry.shape)
    carry_ref[...] = carry
    acc_ref[...] = acc_t

    @pl.when(j == pl.num_programs(1) - 1)
    def _():
        o_ref[0] = acc_t


def _sb_sample(page_table, bias_rows, q_rows, kt_new, vt_new, lt, cache_kt, cache_vt, layer):
    db, n_pages = page_table.shape
    npg = SB_PAGES_PER_STEP
    nj = n_pages // npg
    page_size = cache_kt.shape[3]
    rows = q_rows.shape[1]

    def page_spec(p):
        return pl.BlockSpec((1, 1, KV_DIM, page_size),
                            lambda b, j, pt: (layer, pt[b, (nj - 1 - j) * npg + p], 0, 0))

    per_b = lambda b, j, pt: (b, 0, 0)
    fixed = lambda b, j, pt: (0, 0)
    grid_spec = pltpu.PrefetchScalarGridSpec(
        num_scalar_prefetch=1,
        grid=(db, nj),
        in_specs=[pl.BlockSpec((rows, LANES), fixed),
                  pl.BlockSpec((1, rows, HEAD_DIM), per_b),
                  pl.BlockSpec((1, KV_DIM, page_size), per_b),
                  pl.BlockSpec((1, KV_DIM, page_size), per_b),
                  pl.BlockSpec(lt.shape, fixed)]
                 + [page_spec(p) for p in range(npg)] + [page_spec(p) for p in range(npg)],
        out_specs=pl.BlockSpec((1, KV_DIM, rows), per_b),
        scratch_shapes=[pltpu.VMEM((rows, LANES), F32), pltpu.VMEM((KV_DIM, rows), F32)],
    )
    acc_t = pl.pallas_call(
        functools.partial(_sb_sample_kernel, npg=npg, t_new=rows // N_Q_HEADS),
        grid_spec=grid_spec,
        out_shape=jax.ShapeDtypeStruct((db, KV_DIM, rows), F32),
        compiler_params=_cparams("parallel", "arbitrary"),
        name="sb_sample",
    )(page_table, bias_rows, q_rows, kt_new, vt_new, lt, *([cache_kt] * npg), *([cache_vt] * npg))
    per_head = rows // N_KV_HEADS
    x = acc_t.reshape(db, N_KV_HEADS, HEAD_DIM, N_KV_HEADS, per_head)
    heads = jnp.arange(N_KV_HEADS)
    return x[:, heads, :, heads, :].transpose(1, 0, 3, 2).reshape(db, rows, HEAD_DIM).astype(BF16)


def _ln_silu(c, g, b):
    mu = jnp.mean(c, axis=-1, keepdims=True)
    cc = c - mu
    var = jnp.mean(cc * cc, axis=-1, keepdims=True)
    y = cc * lax.rsqrt(var + LN_EPS) * g + b
    return y * (1.0 / (1.0 + jnp.exp(-y)))


def _conv_prompt_kernel(prev_ref, cur_ref, w_ref, b_ref, g_ref, be_ref, o_ref, ext_ref, *, tiles_per_seq):
    i = pl.program_id(0)
    tm = cur_ref.shape[0]
    halo = prev_ref.shape[0]
    first = (i % tiles_per_seq) == 0
    ext_ref[0:halo, :] = jnp.where(first, 0.0, prev_ref[...])
    ext_ref[halo:halo + tm, :] = cur_ref[...]
    base = halo - (CONV_WIDTH - 1)
    acc = jnp.zeros((tm, cur_ref.shape[1]), F32) + b_ref[...]
    for w in range(CONV_WIDTH):
        acc = acc + ext_ref[base + w:base + w + tm, :] * w_ref[w:w + 1, :]
    o_ref[...] = _ln_silu(acc, g_ref[...], be_ref[...]).astype(BF16)


def _conv_prompt(u, w_dw, b_dw, ln_g, ln_b, n_prompt, seq):
    d = u.shape[1]
    tm, halo = CONV_TILE, CONV_HALO
    ratio = tm // halo
    fixed = lambda i: (0, 0)
    vec = lambda a: a.reshape(1, d)
    return pl.pallas_call(
        functools.partial(_conv_prompt_kernel, tiles_per_seq=seq // tm),
        grid=(n_prompt // tm,),
        in_specs=[pl.BlockSpec((halo, d), lambda i: (jnp.maximum(i * ratio - 1, 0), 0)),
                  pl.BlockSpec((tm, d), lambda i: (i, 0)),
                  pl.BlockSpec((CONV_WIDTH, d), fixed),
                  pl.BlockSpec((1, d), fixed), pl.BlockSpec((1, d), fixed), pl.BlockSpec((1, d), fixed)],
        out_specs=pl.BlockSpec((tm, d), lambda i: (i, 0)),
        out_shape=jax.ShapeDtypeStruct((n_prompt, d), BF16),
        scratch_shapes=[pltpu.VMEM((halo + tm, d), F32)],
        compiler_params=_cparams("parallel"),
        name="conv_prompt",
    )(u, u, w_dw, vec(b_dw), vec(ln_g), vec(ln_b))


def _conv_sample_kernel(st_ref, u_ref, wst_ref, wu_ref, b_ref, g_ref, be_ref, o_ref):
    st, u = st_ref[...], u_ref[...]
    t_new = u.shape[1]
    for t in range(t_new):
        c = (jnp.sum(st * wst_ref[t][None], axis=1) + jnp.sum(u * wu_ref[t][None], axis=1) + b_ref[...])
        o_ref[t] = _ln_silu(c, g_ref[...], be_ref[...]).astype(BF16)


def _conv_sample(state, u_s, w_dw, b_dw, ln_g, ln_b):
    db, n_left, d = state.shape
    t_new = u_s.shape[1]
    bb = CONV_SAMPLE_BATCH
    r = jnp.arange(n_left)[None, :] - jnp.arange(t_new)[:, None]
    wst = jnp.where((r >= 0)[..., None], w_dw[jnp.clip(r, 0, CONV_WIDTH - 1)], 0.0)
    s = n_left - jnp.arange(t_new)[:, None] + jnp.arange(t_new)[None, :]
    wu = jnp.where((s <= CONV_WIDTH - 1)[..., None], w_dw[jnp.clip(s, 0, CONV_WIDTH - 1)], 0.0)
    fixed2 = lambda i: (0, 0)
    fixed3 = lambda i: (0, 0, 0)
    vec = lambda a: a.reshape(1, d)
    out = pl.pallas_call(
        _conv_sample_kernel,
        grid=(db // bb,),
        in_specs=[pl.BlockSpec((bb, n_left, d), lambda i: (i, 0, 0)),
                  pl.BlockSpec((bb, t_new, d), lambda i: (i, 0, 0)),
                  pl.BlockSpec((t_new, n_left, d), fixed3), pl.BlockSpec((t_new, t_new, d), fixed3),
                  pl.BlockSpec((1, d), fixed2), pl.BlockSpec((1, d), fixed2), pl.BlockSpec((1, d), fixed2)],
        out_specs=pl.BlockSpec((t_new, bb, d), lambda i: (0, i, 0)),
        out_shape=jax.ShapeDtypeStruct((t_new, db, d), BF16),
        compiler_params=_cparams("parallel"),
        name="conv_sample",
    )(state, u_s, wst, wu, vec(b_dw), vec(ln_g), vec(ln_b))
    return out.transpose(1, 0, 2).reshape(db * t_new, d)


def _swa_prompt_kernel(sink_ref, q_ref, kp_ref, kc_ref, vp_ref, vc_ref, o_ref):
    i = pl.program_id(1)
    w = q_ref.shape[0]
    q = q_ref[...]
    k2 = jnp.concatenate([kp_ref[...], kc_ref[...]], axis=0)
    v2 = jnp.concatenate([vp_ref[...], vc_ref[...]], axis=0)
    rows = GROUP * w
    r = lax.broadcasted_iota(jnp.int32, (rows, 2 * w), 0)
    c = lax.broadcasted_iota(jnp.int32, (rows, 2 * w), 1)
    rel = (r & (w - 1)) - c + w
    first_valid = jnp.where(i > 0, 0, w)
    mask = (rel >= 0) & (rel <= WINDOW) & (c >= first_valid)
    outs = []
    for h in range(N_KV_HEADS):
        qh = jnp.concatenate(
            [q[:, (h * GROUP + g) * HEAD_DIM:(h * GROUP + g + 1) * HEAD_DIM] for g in range(GROUP)], axis=0)
        z = _dot_nt(qh, k2[:, h * HEAD_DIM:(h + 1) * HEAD_DIM])
        z = jnp.where(mask, z, -jnp.inf)
        s = jnp.concatenate([jnp.full((w, 1), sink_ref[h * GROUP + g], F32) for g in range(GROUP)], axis=0)
        m = jnp.maximum(jnp.max(z, axis=-1, keepdims=True), s)
        e = jnp.exp(z - m)
        den = jnp.sum(e, axis=-1, keepdims=True) + jnp.exp(s - m)
        o = _dot(e.astype(BF16), v2[:, h * HEAD_DIM:(h + 1) * HEAD_DIM]) / den
        outs += [o[g * w:(g + 1) * w] for g in range(GROUP)]
    o_ref[...] = jnp.concatenate(outs, axis=1).astype(BF16)


def _swa_prompt(sinks, q, k, v, batch, seq):
    w = WINDOW
    nb = seq // w
    cur = lambda b, i: (b * nb + i, 0)
    prev = lambda b, i: (b * nb + jnp.maximum(i - 1, 0), 0)
    return pl.pallas_call(
        _swa_prompt_kernel,
        grid=(batch, nb),
        in_specs=[pl.BlockSpec(memory_space=pltpu.SMEM),
                  pl.BlockSpec((w, Q_DIM), cur),
                  pl.BlockSpec((w, KV_DIM), prev), pl.BlockSpec((w, KV_DIM), cur),
                  pl.BlockSpec((w, KV_DIM), prev), pl.BlockSpec((w, KV_DIM), cur)],
        out_specs=pl.BlockSpec((w, Q_DIM), cur),
        out_shape=jax.ShapeDtypeStruct((batch * seq, Q_DIM), BF16),
        compiler_params=_cparams("parallel", "parallel"),
        name="swa_prompt",
    )(sinks, q, k, k, v, v)


def _swa_sample_kernel(sink_ref, q_ref, kb_ref, vb_ref, kn_ref, vn_ref, o_ref, *, t_new):
    qbd = _block_diag_q(q_ref[0])
    rows = qbd.shape[0]
    wb = kb_ref.shape[1]
    kn = _pad_rows(kn_ref[0], wb).astype(BF16)
    vn = _pad_rows(vn_ref[0], wb).astype(BF16)
    r = lax.broadcasted_iota(jnp.int32, (rows, wb), 0)
    c = lax.broadcasted_iota(jnp.int32, (rows, wb), 1)
    t = _div_pow2(r, GROUP) & (t_new - 1)
    z1 = jnp.where((t + wb - c >= 0) & (t + wb - c <= WINDOW), _dot_nt(qbd, kb_ref[0].astype(BF16)), -jnp.inf)
    z2 = jnp.where((t - c >= 0) & (c < t_new), _dot_nt(qbd, kn), -jnp.inf)
    s = sink_ref[:, 0:1]
    m = jnp.maximum(jnp.maximum(jnp.max(z1, axis=-1, keepdims=True), jnp.max(z2, axis=-1, keepdims=True)), s)
    e1, e2 = jnp.exp(z1 - m), jnp.exp(z2 - m)
    den = jnp.sum(e1, axis=-1, keepdims=True) + jnp.sum(e2, axis=-1, keepdims=True) + jnp.exp(s - m)
    o = (_dot(e1.astype(BF16), vb_ref[0].astype(BF16)) + _dot(e2.astype(BF16), vn)) / den
    o_ref[0] = _diag_blocks(o).astype(BF16)


def _swa_sample(sink_rows, q_rows, k_buf, v_buf, k_new, v_new):
    db, wb, _ = k_buf.shape
    rows = q_rows.shape[1]
    t_pad = k_new.shape[1]
    per_b = lambda b: (b, 0, 0)
    return pl.pallas_call(
        functools.partial(_swa_sample_kernel, t_new=rows // N_Q_HEADS),
        grid=(db,),
        in_specs=[pl.BlockSpec((rows, LANES), lambda b: (0, 0)),
                  pl.BlockSpec((1, rows, HEAD_DIM), per_b),
                  pl.BlockSpec((1, wb, KV_DIM), per_b), pl.BlockSpec((1, wb, KV_DIM), per_b),
                  pl.BlockSpec((1, t_pad, KV_DIM), per_b), pl.BlockSpec((1, t_pad, KV_DIM), per_b)],
        out_specs=pl.BlockSpec((1, rows, HEAD_DIM), per_b),
        out_shape=jax.ShapeDtypeStruct((db, rows, HEAD_DIM), BF16),
        compiler_params=_cparams("parallel"),
        name="swa_sample",
    )(sink_rows, q_rows, k_buf, v_buf, k_new, v_new)


def _sample_q_rows(q_s, db, t):
    return q_s.reshape(db, t, N_KV_HEADS, GROUP, HEAD_DIM).transpose(0, 2, 1, 3, 4).reshape(
        db, N_Q_HEADS * t, HEAD_DIM)


def _sample_o_tokens(o_rows, db, t):
    return o_rows.reshape(db, N_KV_HEADS, t, GROUP, HEAD_DIM).transpose(0, 2, 1, 3, 4).reshape(db * t, Q_DIM)


def _head_rows(per_head, t):
    rows = jnp.broadcast_to(per_head.reshape(N_KV_HEADS, 1, GROUP), (N_KV_HEADS, t, GROUP)).reshape(-1, 1)
    return jnp.broadcast_to(rows, (rows.shape[0], LANES)).astype(F32)


def _pad_new(x, db, t):
    return jnp.pad(x.reshape(db, t, KV_DIM), ((0, 0), (0, 8 - t), (0, 0)))


def _new_keys_on_lanes(x, db, t, width):
    xt = x.reshape(db, t, KV_DIM).transpose(0, 2, 1).astype(BF16)
    return jnp.pad(xt, ((0, 0), (0, 0), (0, width - t)))


def _rope_tables(pos):
    half = ROPE_DIM // 2
    inv = ROPE_THETA ** (-jnp.arange(half, dtype=F32) / half)
    ang = pos.astype(F32)[:, None] * inv[None, :]
    cos, sin = jnp.cos(ang), jnp.sin(ang)
    n = pos.shape[0]
    rest = HEAD_DIM - ROPE_DIM
    c = jnp.concatenate([cos, cos, jnp.ones((n, rest), F32)], axis=1)
    sa = jnp.concatenate([-sin, jnp.zeros((n, half + rest), F32)], axis=1)
    sb = jnp.concatenate([jnp.zeros((n, half), F32), sin, jnp.zeros((n, rest), F32)], axis=1)
    rep = LANES // HEAD_DIM
    return tuple(jnp.concatenate([x] * rep, axis=1) for x in (c, sa, sb))


def kernel(x_prompt, x_sample, cache_sb_k, cache_sb_v, state_conv, cache_swa_k, cache_swa_v, page_table,
           norm_mix, norm_ffn, w_ffn_up, w_ffn_down, w_sb_qkv, w_sb_o, sb_bias,
           w_cv_pw1, b_cv_pw1, w_cv_dw, b_cv_dw, cv_ln_g, cv_ln_b, w_cv_pw2, b_cv_pw2,
           w_swa_qkv, b_swa_qkv, swa_sinks, w_swa_o, b_swa_o, norm_final):
    batch, seq, d = x_prompt.shape
    db, t_new, _ = x_sample.shape
    depth = norm_mix.shape[0]
    past_len = page_table.shape[1] * cache_sb_k.shape[2]
    n_p, n_s = batch * seq, db * t_new
    assert (n_p + n_s) % TOKEN_TILE == 0 and n_p % TOKEN_TILE == 0
    assert cache_swa_k.shape[2] == WINDOW and t_new <= 8

    x = jnp.concatenate([x_prompt.reshape(n_p, d), x_sample.reshape(n_s, d)], axis=0)
    n_phys, page = cache_sb_k.shape[1], cache_sb_k.shape[2]
    cache_kt = cache_sb_k.transpose(0, 1, 3, 4, 2).reshape(cache_sb_k.shape[0], n_phys, KV_DIM, page)
    cache_vt = cache_sb_v.transpose(0, 1, 3, 4, 2).reshape(cache_sb_v.shape[0], n_phys, KV_DIM, page)
    jj = jnp.arange(SB_K_BLOCK)
    lt = (jj[:, None] >= jj[None, :]).astype(BF16)
    zero_bias_d = jnp.zeros((d,), F32)

    outs = {k: [] for k in ("sbkp", "sbvp", "sbks", "sbvs", "cvp", "cvs", "swkp", "swvp", "swks", "swvs")}
    a_i = c_i = w_i = 0
    for layer in range(depth):
        kind = layer % N_MIXERS
        if kind == 0:
            w_qkv = w_sb_qkv[a_i].astype(BF16)
            no_bias = jnp.zeros((Q_DIM + 2 * KV_DIM,), F32)
            bias2 = sb_bias[a_i] * LOG2E
            tiles_per_seq = seq // TOKEN_TILE
            blocks_per_tile = TOKEN_TILE // SB_K_BLOCK
            kvt_f32 = (jax.ShapeDtypeStruct((batch, KV_DIM, seq), F32),
                       pl.BlockSpec((1, KV_DIM, TOKEN_TILE), lambda i: (i // tiles_per_seq, 0, i % tiles_per_seq)))
            kvt_blk = (jax.ShapeDtypeStruct((n_p // SB_K_BLOCK, KV_DIM, SB_K_BLOCK), BF16),
                       pl.BlockSpec((blocks_per_tile, KV_DIM, SB_K_BLOCK), lambda i: (i, 0, 0)))
            q_p, ktf, vtf, ktb, vtb = _norm_linear(
                x, norm_mix[layer], w_qkv, no_bias, _sb_qkv_prompt_epilogue,
                [_row_out(n_p, Q_DIM, BF16), kvt_f32, kvt_f32, kvt_blk, kvt_blk],
                n_tiles=n_p // TOKEN_TILE, name="sb_qkv_prompt")
            q_s, k_s, v_s = _norm_linear(
                x, norm_mix[layer], w_qkv, no_bias, _sb_qkv_epilogue,
                [_row_out(n_s, Q_DIM, BF16), _row_out(n_s, KV_DIM, F32), _row_out(n_s, KV_DIM, F32)],
                first_tile=n_p // TOKEN_TILE, name="sb_qkv_sample")
            o_p = _sb_prompt(bias2, q_p, ktb, vtb, lt, batch, seq)
            o_s = _sb_sample(page_table, _head_rows(bias2, t_new), _sample_q_rows(q_s, db, t_new),
                             _new_keys_on_lanes(k_s, db, t_new, page), _new_keys_on_lanes(v_s, db, t_new, page),
                             lt, cache_kt, cache_vt, a_i)
            o = jnp.concatenate([o_p, _sample_o_tokens(o_s, db, t_new)], axis=0)
            w_o, b_o = w_sb_o[a_i], zero_bias_d
            heads_t = lambda t: t.reshape(batch, N_KV_HEADS, HEAD_DIM, seq).transpose(0, 3, 1, 2)
            outs["sbkp"].append(heads_t(ktf))
            outs["sbvp"].append(heads_t(vtf))
            outs["sbks"].append(k_s.reshape(db, t_new, N_KV_HEADS, HEAD_DIM))
            outs["sbvs"].append(v_s.reshape(db, t_new, N_KV_HEADS, HEAD_DIM))
            a_i += 1
        elif kind == 1:
            (u,) = _norm_linear(x, norm_mix[layer], w_cv_pw1[c_i].astype(BF16), b_cv_pw1[c_i],
                                _glu_epilogue, [_row_out(n_p + n_s, d, F32)], name="conv_pw1")
            u_p, u_s = u[:n_p].reshape(batch, seq, d), u[n_p:].reshape(db, t_new, d)
            c_p = _conv_prompt(u, w_cv_dw[c_i], b_cv_dw[c_i], cv_ln_g[c_i], cv_ln_b[c_i], n_p, seq)
            c_s = _conv_sample(state_conv[c_i], u_s, w_cv_dw[c_i], b_cv_dw[c_i], cv_ln_g[c_i], cv_ln_b[c_i])
            o = jnp.concatenate([c_p, c_s], axis=0)
            w_o, b_o = w_cv_pw2[c_i], b_cv_pw2[c_i]
            n_left = CONV_WIDTH - 1
            outs["cvp"].append(u_p[:, seq - n_left:])
            outs["cvs"].append(jnp.concatenate([state_conv[c_i], u_s], axis=1)[:, -n_left:])
            c_i += 1
        else:
            pos = jnp.concatenate([jnp.tile(jnp.arange(seq, dtype=jnp.int32), batch),
                                   jnp.tile(past_len + jnp.arange(t_new, dtype=jnp.int32), db)])
            q, k, v = _norm_linear(
                x, norm_mix[layer], w_swa_qkv[w_i].astype(BF16), b_swa_qkv[w_i],
                _swa_qkv_epilogue,
                [_row_out(n_p + n_s, Q_DIM, BF16), _row_out(n_p + n_s, KV_DIM, F32),
                 _row_out(n_p + n_s, KV_DIM, F32)],
                extra=_rope_tables(pos), name="swa_qkv")
            kp, vp = k[:n_p], v[:n_p]
            o_p = _swa_prompt(swa_sinks[w_i], q, kp.astype(BF16), vp.astype(BF16), batch, seq)
            k_new, v_new = k[n_p:].reshape(db, t_new, KV_DIM), v[n_p:].reshape(db, t_new, KV_DIM)
            k_buf = cache_swa_k[w_i].reshape(db, WINDOW, KV_DIM)
            v_buf = cache_swa_v[w_i].reshape(db, WINDOW, KV_DIM)
            o_s = _swa_sample(_head_rows(swa_sinks[w_i], t_new), _sample_q_rows(q[n_p:], db, t_new),
                              k_buf, v_buf, _pad_new(k[n_p:], db, t_new), _pad_new(v[n_p:], db, t_new))
            o = jnp.concatenate([o_p, _sample_o_tokens(o_s, db, t_new)], axis=0)
            w_o, b_o = w_swa_o[w_i], b_swa_o[w_i]
            heads = (N_KV_HEADS, HEAD_DIM)
            outs["swkp"].append(kp.reshape(batch, seq, *heads)[:, seq - WINDOW:])
            outs["swvp"].append(vp.reshape(batch, seq, *heads)[:, seq - WINDOW:])
            outs["swks"].append(jnp.concatenate([k_buf, k_new], axis=1)[:, -WINDOW:].reshape(db, WINDOW, *heads))
            outs["swvs"].append(jnp.concatenate([v_buf, v_new], axis=1)[:, -WINDOW:].reshape(db, WINDOW, *heads))
            w_i += 1
        x = _proj_mlp(o, w_o.astype(BF16), b_o, x, norm_ffn[layer], w_ffn_up[layer].astype(BF16),
                      w_ffn_down[layer].astype(BF16), norm_final, final_norm=(layer == depth - 1))

    y_prompt = x[:n_p].reshape(batch, seq, d)
    y_sample = x[n_p:].reshape(db, t_new, d)
    return (y_prompt, y_sample,
            jnp.stack(outs["sbkp"]), jnp.stack(outs["sbvp"]), jnp.stack(outs["sbks"]), jnp.stack(outs["sbvs"]),
            jnp.stack(outs["cvp"]), jnp.stack(outs["cvs"]),
            jnp.stack(outs["swkp"]), jnp.stack(outs["swvp"]), jnp.stack(outs["swks"]), jnp.stack(outs["swvs"]))
```

```python
import functools

import jax
import jax.numpy as jnp
from jax import lax
from jax.experimental import pallas as pl
from jax.experimental.pallas import tpu as pltpu

F32 = jnp.float32
BF16 = jnp.bfloat16

HEAD_DIM = 64
N_Q_HEADS = 16
N_KV_HEADS = 4
GROUP = N_Q_HEADS // N_KV_HEADS
Q_DIM = N_Q_HEADS * HEAD_DIM
KV_DIM = N_KV_HEADS * HEAD_DIM
N_MIXERS = 3
CONV_WIDTH = 31
WINDOW = 128
ROPE_THETA = 500000.0
ROPE_DIM = HEAD_DIM // 4
RMS_EPS = 1e-6
LN_EPS = 1e-5
QK_SCALE = HEAD_DIM ** -0.5
LOG2E = 1.4426950408889634
MASKED_LOGIT = -1e30

LANES = 128
VMEM_LIMIT = 56 * 1024 * 1024

TOKEN_TILE = 512
FF_TILE = 1024
SB_Q_TILE = 128
SB_K_BLOCK = 256
SB_BLOCKS_PER_TRIP = 1
SB_LAG = 3
SB_PAGES_PER_STEP = 16
CONV_TILE = 256
CONV_HALO = 32
CONV_SAMPLE_BATCH = 8


def _cparams(*sem, flags=None):
    return pltpu.CompilerParams(dimension_semantics=sem, vmem_limit_bytes=VMEM_LIMIT, flags=flags)


def _rms(x, g):
    ms = jnp.mean(x * x, axis=-1, keepdims=True)
    return x * lax.rsqrt(ms + RMS_EPS) * g


def _softplus2(y):
    sign = jnp.uint32(0x80000000)
    neg_abs = lax.bitcast_convert_type(lax.bitcast_convert_type(y, jnp.uint32) | sign, F32)
    return jnp.maximum(y, 0.0) + jnp.log(1.0 + jnp.exp2(neg_abs)) * LOG2E


def _div_pow2(x, n):
    assert n & (n - 1) == 0
    return x >> (n.bit_length() - 1)


def _dot(a, b):
    return jnp.dot(a, b, preferred_element_type=F32)


def _dot_nt(a, b):
    return lax.dot_general(a, b, (((1,), (1,)), ((), ())), preferred_element_type=F32)


def _sb_qkv_epilogue(acc, extra, outs):
    q_ref, k_ref, v_ref = outs
    q_ref[...] = (acc[:, :Q_DIM] * (QK_SCALE * LOG2E)).astype(BF16)
    k_ref[...] = acc[:, Q_DIM:Q_DIM + KV_DIM]
    v_ref[...] = acc[:, Q_DIM + KV_DIM:]


def _sb_qkv_prompt_epilogue(acc, extra, outs):
    q_ref, ktf_ref, vtf_ref, ktb_ref, vtb_ref = outs
    q_ref[...] = (acc[:, :Q_DIM] * (QK_SCALE * LOG2E)).astype(BF16)
    bk = ktb_ref.shape[2]
    for lo, f_ref, b_ref in ((Q_DIM, ktf_ref, ktb_ref), (Q_DIM + KV_DIM, vtf_ref, vtb_ref)):
        t = acc[:, lo:lo + KV_DIM].T
        f_ref[0] = t
        for blk in range(b_ref.shape[0]):
            b_ref[blk] = t[:, blk * bk:(blk + 1) * bk].astype(BF16)


def _rope_group(x, c, sa, sb):
    return x * c + pltpu.roll(x, LANES - ROPE_DIM // 2, 1) * sa + pltpu.roll(x, ROPE_DIM // 2, 1) * sb


def _swa_qkv_epilogue(acc, extra, outs):
    c_ref, sa_ref, sb_ref = extra
    q_ref, k_ref, v_ref = outs
    c, sa, sb = c_ref[...], sa_ref[...], sb_ref[...]
    for j in range(Q_DIM // LANES):
        x = acc[:, j * LANES:(j + 1) * LANES]
        q_ref[:, j * LANES:(j + 1) * LANES] = (_rope_group(x, c, sa, sb) * QK_SCALE).astype(BF16)
    for j in range(KV_DIM // LANES):
        x = acc[:, Q_DIM + j * LANES:Q_DIM + (j + 1) * LANES]
        k_ref[:, j * LANES:(j + 1) * LANES] = _rope_group(x, c, sa, sb)
    v_ref[...] = acc[:, Q_DIM + KV_DIM:]


def _glu_epilogue(acc, extra, outs):
    (u_ref,) = outs
    d = acc.shape[1] // 2
    u_ref[...] = acc[:, :d] * (1.0 / (1.0 + jnp.exp(-acc[:, d:])))


def _norm_linear_kernel(x_ref, g_ref, w_ref, b_ref, *refs, n_extra, epilogue):
    h = _rms(x_ref[...], g_ref[...]).astype(BF16)
    acc = _dot(h, w_ref[...]) + b_ref[...]
    epilogue(acc, refs[:n_extra], refs[n_extra:])


def _row_out(n, cols, dtype):
    return (jax.ShapeDtypeStruct((n, cols), dtype), pl.BlockSpec((TOKEN_TILE, cols), lambda i: (i, 0)))


def _norm_linear(x, g, w, b, epilogue, outs, extra=(), first_tile=0, n_tiles=None, name="norm_linear"):
    n, d = x.shape
    n_out = w.shape[1]
    tm = TOKEN_TILE
    n_tiles = n // tm - first_tile if n_tiles is None else n_tiles
    row = lambda i: (i + first_tile, 0)
    fixed = lambda i: (0, 0)
    in_specs = [pl.BlockSpec((tm, d), row), pl.BlockSpec((1, d), fixed),
                pl.BlockSpec((d, n_out), fixed), pl.BlockSpec((1, n_out), fixed)]
    in_specs += [pl.BlockSpec((tm, e.shape[1]), row) for e in extra]
    return pl.pallas_call(
        functools.partial(_norm_linear_kernel, n_extra=len(extra), epilogue=epilogue),
        grid=(n_tiles,),
        in_specs=in_specs,
        out_specs=[spec for _, spec in outs],
        out_shape=[shape for shape, _ in outs],
        compiler_params=_cparams("parallel"),
        name=name,
    )(x, g.reshape(1, d), w, b.reshape(1, n_out), *extra)


def _proj_mlp_kernel(op_ref, os_ref, wo_ref, bo_ref, x_ref, g_ref, wu_ref, wd_ref, gf_ref, *refs,
                     prompt_tiles, final_norm):
    out_refs, (acc_ref, h_ref) = refs[:-2], refs[-2:]
    i = pl.program_id(0)
    k = pl.program_id(1)
    is_prompt = i < prompt_tiles

    def start(o_ref):
        x1 = x_ref[...] + _dot(o_ref[...], wo_ref[...]) + bo_ref[...]
        acc_ref[...] = x1
        h_ref[...] = _rms(x1, g_ref[...]).astype(BF16)

    @pl.when(jnp.logical_and(k == 0, is_prompt))
    def _():
        start(op_ref)

    @pl.when(jnp.logical_and(k == 0, jnp.logical_not(is_prompt)))
    def _():
        start(os_ref)

    u = jnp.maximum(_dot(h_ref[...], wu_ref[...]), 0.0)
    acc_ref[...] += _dot((u * u).astype(BF16), wd_ref[...])
    last = k == pl.num_programs(1) - 1
    if final_norm:
        yp_ref, ys_ref = out_refs

        @pl.when(jnp.logical_and(last, is_prompt))
        def _():
            yp_ref[...] = _rms(acc_ref[...], gf_ref[...])

        @pl.when(jnp.logical_and(last, jnp.logical_not(is_prompt)))
        def _():
            ys_ref[...] = _rms(acc_ref[...], gf_ref[...])
    else:
        @pl.when(last)
        def _():
            out_refs[0][...] = acc_ref[...]


def _proj_mlp(o_p, o_s, wo, bo, x, g, wu, wd, gf, final_norm):
    n, d = x.shape
    n_p, n_s = o_p.shape[0], o_s.shape[0]
    d_ff = wu.shape[1]
    tm, tf = TOKEN_TILE, FF_TILE
    prompt_tiles = n_p // tm
    row = lambda i, k: (i, 0)
    fixed = lambda i, k: (0, 0)
    prompt_row = lambda i, k: (jnp.minimum(i, prompt_tiles - 1), 0)
    sample_row = lambda i, k: (jnp.maximum(i - prompt_tiles, 0), 0)
    if final_norm:
        out_specs = [pl.BlockSpec((tm, d), prompt_row), pl.BlockSpec((tm, d), sample_row)]
        out_shape = [jax.ShapeDtypeStruct((n_p, d), F32), jax.ShapeDtypeStruct((n_s, d), F32)]
    else:
        out_specs = pl.BlockSpec((tm, d), row)
        out_shape = jax.ShapeDtypeStruct((n, d), F32)
    return pl.pallas_call(
        functools.partial(_proj_mlp_kernel, prompt_tiles=prompt_tiles, final_norm=final_norm),
        grid=(n // tm, d_ff // tf),
        in_specs=[pl.BlockSpec((tm, d), prompt_row), pl.BlockSpec((tm, d), sample_row),
                  pl.BlockSpec((d, d), fixed), pl.BlockSpec((1, d), fixed),
                  pl.BlockSpec((tm, d), row), pl.BlockSpec((1, d), fixed),
                  pl.BlockSpec((d, tf), lambda i, k: (0, k)), pl.BlockSpec((tf, d), lambda i, k: (k, 0)),
                  pl.BlockSpec((1, d), fixed)],
        out_specs=out_specs,
        out_shape=out_shape,
        scratch_shapes=[pltpu.VMEM((tm, d), F32), pltpu.VMEM((tm, d), BF16)],
        compiler_params=_cparams("parallel", "arbitrary"),
        name="proj_mlp",
    )(o_p, o_s, wo, bo.reshape(1, d), x, g.reshape(1, d), wu, wd, gf.reshape(1, d))


def _sb_prompt_kernel(bias_ref, q_ref, kt_ref, vt_ref, lt_ref, o_ref,
                      raw_scr, y_scr, tail_scr, a_scr, carry_scr, acc_scr, *, tq, bk):
    kv = pl.program_id(1)
    rows = GROUP * tq
    nh = y_scr.shape[1] // bk
    wide = nh * bk
    nq = q_ref.shape[0] // tq
    cols = lambda h: slice(h * bk, (h + 1) * bk)
    diag_group = lambda i: (i * tq) // wide
    n_trips = lambda i: jnp.maximum(diag_group(i) + 1, SB_LAG)

    def load_q(i):
        q = q_ref[pl.ds(pl.multiple_of(i * tq, tq), tq), :]
        return jnp.concatenate([q[:, j * HEAD_DIM:(j + 1) * HEAD_DIM] for j in range(GROUP)], axis=0)

    def trip(q_all, score_group, logits, weights, value_group):
        raws = av = None
        if q_all is not None:
            raws = [_dot(q_all, kt_ref[nh * score_group + h]) for h in range(nh)]
        if value_group is not None:
            av = _dot_nt(a_scr[:, cols(0)], vt_ref[nh * value_group])
            for h in range(1, nh):
                av = av + _dot_nt(a_scr[:, cols(h)], vt_ref[nh * value_group + h])
        if weights:
            c = carry_scr[...]
            for h in reversed(range(nh)):
                tail = tail_scr[:, cols(h)]
                w = y_scr[:, cols(h)] - tail - jnp.concatenate([c] * (bk // LANES), axis=1)
                a_scr[:, cols(h)] = jnp.exp2(w).astype(BF16)
                c = c + jnp.broadcast_to(tail[:, 0:1], c.shape)
            carry_scr[...] = c
        if logits is not None:
            for h in range(nh):
                y = raw_scr[:, cols(h)]
                y = jnp.concatenate(
                    [y[j * tq:(j + 1) * tq] + bias_ref[kv * GROUP + j] for j in range(GROUP)], axis=0)
                if logits[0] == "diag":
                    _, tile, group = logits
                    kpos = (nh * group + h) * bk + lax.broadcasted_iota(jnp.int32, (rows, bk), 1)
                    qpos = tile * tq + (lax.broadcasted_iota(jnp.int32, (rows, bk), 0) & (tq - 1))
                    y = jnp.where(kpos < qpos, y, MASKED_LOGIT)
                elif logits[0] == "pad":
                    y = jnp.where(logits[1], MASKED_LOGIT, y)
                y_scr[:, cols(h)] = y
                sp = _softplus2(y).astype(BF16)
                tail_scr[:, cols(h)] = _dot(sp, lt_ref[...])
        if raws is not None:
            for h in range(nh):
                raw_scr[:, cols(h)] = raws[h]
        if av is not None:
            acc_scr[...] += av

    def emit(tile):
        acc = acc_scr[...]
        start = tile * tq if isinstance(tile, int) else pl.multiple_of(tile * tq, tq)
        o_ref[pl.ds(start, tq), :] = jnp.concatenate(
            [acc[j * tq:(j + 1) * tq] for j in range(GROUP)], axis=1).astype(BF16)

    raw_scr[...] = jnp.zeros_like(raw_scr)
    y_scr[...] = jnp.full_like(y_scr, MASKED_LOGIT)
    tail_scr[...] = jnp.zeros_like(tail_scr)
    a_scr[...] = jnp.zeros_like(a_scr)
    carry_scr[...] = jnp.zeros_like(carry_scr)
    acc_scr[...] = jnp.zeros_like(acc_scr)

    def tile_trips(i, _):
        q_all = load_q(i)
        jd = diag_group(i)
        prev = jnp.maximum(i - 1, 0)
        jd_prev, trips_prev = diag_group(prev), n_trips(prev)
        group = lambda n: jnp.maximum(jd - n, 0)
        prev_group = lambda n: jnp.maximum(jd_prev - (trips_prev - SB_LAG + n), 0)
        prev_padded = jnp.logical_or(i == 0, jd_prev + 1 < SB_LAG)
        trip(q_all, group(0), ("pad", prev_padded), True, prev_group(0))
        trip(q_all, group(1), ("diag", i, jd), True, prev_group(1))
        carry_scr[...] = jnp.zeros_like(carry_scr)
        trip(q_all, group(2), ("pad", jd < 1), True, prev_group(2))

        @pl.when(i > 0)
        def _():
            emit(i - 1)

        acc_scr[...] = jnp.zeros_like(acc_scr)

        def steady(n, _):
            trip(q_all, jd - n, ("plain",), True, jd - (n - SB_LAG))
            return 0

        lax.fori_loop(SB_LAG, n_trips(i), steady, 0)
        return 0

    lax.fori_loop(0, nq, tile_trips, 0)

    last = nq - 1
    jd_last = (last * tq) // wide
    trips_last = max(jd_last + 1, SB_LAG)
    last_group = lambda n: max(jd_last - (trips_last - SB_LAG + n), 0)
    trip(None, None, ("pad", True) if jd_last + 1 < SB_LAG else ("plain",), True, last_group(0))
    trip(None, None, None, True, last_group(1))
    trip(None, None, None, False, last_group(2))
    emit(last)


def _sb_prompt(bias2, q, ktb, vtb, lt, batch, seq):
    tq, bk = SB_Q_TILE, SB_K_BLOCK
    nb = seq // bk
    rows = GROUP * tq
    wide = SB_BLOCKS_PER_TRIP * bk
    assert seq % wide == 0 and seq % tq == 0
    kv_spec = pl.BlockSpec((nb, HEAD_DIM, bk), lambda b, h: (b, h, 0))
    q_spec = pl.BlockSpec((seq, GROUP * HEAD_DIM), lambda b, h: (b, h))
    return pl.pallas_call(
        functools.partial(_sb_prompt_kernel, tq=tq, bk=bk),
        grid=(batch, N_KV_HEADS),
        in_specs=[pl.BlockSpec(memory_space=pltpu.SMEM), q_spec, kv_spec, kv_spec,
                  pl.BlockSpec((bk, bk), lambda b, h: (0, 0))],
        out_specs=q_spec,
        out_shape=jax.ShapeDtypeStruct((batch * seq, Q_DIM), BF16),
        scratch_shapes=[pltpu.VMEM((rows, wide), F32), pltpu.VMEM((rows, wide), F32),
                        pltpu.VMEM((rows, wide), F32), pltpu.VMEM((rows, wide), BF16),
                        pltpu.VMEM((rows, LANES), F32), pltpu.VMEM((rows, HEAD_DIM), F32)],
        compiler_params=_cparams("parallel", "arbitrary"),
        name="sb_prompt",
    )(bias2, q, ktb, vtb, lt)


def _block_diag_q(q):
    q4 = jnp.concatenate([q] * N_KV_HEADS, axis=1)
    r = lax.broadcasted_iota(jnp.int32, q4.shape, 0)
    c = lax.broadcasted_iota(jnp.int32, q4.shape, 1)
    rows_per_head = q.shape[0] // N_KV_HEADS
    return jnp.where(_div_pow2(r, rows_per_head) == _div_pow2(c, HEAD_DIM), q4, jnp.zeros_like(q4))


def _diag_blocks(o):
    rows_per_head = o.shape[0] // N_KV_HEADS
    return jnp.concatenate(
        [o[h * rows_per_head:(h + 1) * rows_per_head, h * HEAD_DIM:(h + 1) * HEAD_DIM]
         for h in range(N_KV_HEADS)], axis=0)


def _pad_rows(x, rows):
    return jnp.concatenate([x, jnp.zeros((rows - x.shape[0], x.shape[1]), x.dtype)], axis=0)


def _sb_sample_block(qbd, bias, kt, vt, lt, carry, acc_t, mask):
    y = _dot(qbd, kt) + bias
    sp = _softplus2(y)
    if mask is not None:
        sp = jnp.where(mask, sp, 0.0)
    tail = _dot(sp.astype(BF16), lt)
    a = jnp.exp2(y - tail - jnp.concatenate([carry] * (y.shape[1] // LANES), axis=1))
    if mask is not None:
        a = jnp.where(mask, a, 0.0)
    acc_t = acc_t + _dot_nt(vt, a.astype(BF16))
    carry = carry + jnp.broadcast_to(tail[:, 0:1], carry.shape)
    return carry, acc_t


def _sb_sample_kernel(pt_ref, bias_ref, q_ref, kn_ref, vn_ref, lt_ref, *refs, npg, t_new):
    k_refs, v_refs = refs[:npg], refs[npg:2 * npg]
    o_ref, carry_ref, acc_ref = refs[2 * npg:]
    j = pl.program_id(1)
    bk = lt_ref.shape[0]
    page = k_refs[0].shape[3]
    qbd = _block_diag_q(q_ref[0])
    rows = qbd.shape[0]
    bias1 = bias_ref[...]
    bias = jnp.concatenate([bias1] * (bk // LANES), axis=1)
    lt = lt_ref[...]

    @pl.when(j == 0)
    def _():
        r = lax.broadcasted_iota(jnp.int32, (rows, page), 0)
        c = lax.broadcasted_iota(jnp.int32, (rows, page), 1)
        t = _div_pow2(r, GROUP) & (t_new - 1)
        carry, acc_t = _sb_sample_block(
            qbd, bias1, kn_ref[0], vn_ref[0], lt[:page, :page],
            jnp.zeros((rows, LANES), F32), jnp.zeros((KV_DIM, rows), F32), c < t)
        carry_ref[...] = carry
        acc_ref[...] = acc_t

    carry, acc_t = carry_ref[...], acc_ref[...]
    pages_per_block = bk // page
    blocks = [range(b * pages_per_block, (b + 1) * pages_per_block)
              for b in reversed(range(npg // pages_per_block))]
    cat = lambda refs, sel: jnp.concatenate([refs[p][0, 0] for p in sel], axis=1).astype(BF16)
    ys = [_dot(qbd, cat(k_refs, sel)) + bias for sel in blocks]
    tails = [_dot(_softplus2(y).astype(BF16), lt) for y in ys]
    for y, tail, sel in zip(ys, tails, blocks):
        a = jnp.exp2(y - tail - jnp.concatenate([carry] * (bk // LANES), axis=1))
        acc_t = acc_t + _dot_nt(cat(v_refs, sel), a.astype(BF16))
        carry = carry + jnp.broadcast_to(tail[:, 0:1], carry.shape)
    carry_ref[...] = carry
    acc_ref[...] = acc_t

    @pl.when(j == pl.num_programs(1) - 1)
    def _():
        o_ref[0] = acc_t


def _sb_sample(page_table, bias_rows, q_rows, kt_new, vt_new, lt, cache_kt, cache_vt, layer):
    db, n_pages = page_table.shape
    npg = SB_PAGES_PER_STEP
    nj = n_pages // npg
    page_size = cache_kt.shape[3]
    rows = q_rows.shape[1]

    def page_spec(p):
        return pl.BlockSpec((1, 1, KV_DIM, page_size),
                            lambda b, j, pt: (layer, pt[b, (nj - 1 - j) * npg + p], 0, 0))

    per_b = lambda b, j, pt: (b, 0, 0)
    fixed = lambda b, j, pt: (0, 0)
    grid_spec = pltpu.PrefetchScalarGridSpec(
        num_scalar_prefetch=1,
        grid=(db, nj),
        in_specs=[pl.BlockSpec((rows, LANES), fixed),
                  pl.BlockSpec((1, rows, HEAD_DIM), per_b),
                  pl.BlockSpec((1, KV_DIM, page_size), per_b),
                  pl.BlockSpec((1, KV_DIM, page_size), per_b),
                  pl.BlockSpec(lt.shape, fixed)]
                 + [page_spec(p) for p in range(npg)] + [page_spec(p) for p in range(npg)],
        out_specs=pl.BlockSpec((1, KV_DIM, rows), per_b),
        scratch_shapes=[pltpu.VMEM((rows, LANES), F32), pltpu.VMEM((KV_DIM, rows), F32)],
    )
    acc_t = pl.pallas_call(
        functools.partial(_sb_sample_kernel, npg=npg, t_new=rows // N_Q_HEADS),
        grid_spec=grid_spec,
        out_shape=jax.ShapeDtypeStruct((db, KV_DIM, rows), F32),
        compiler_params=_cparams("parallel", "arbitrary"),
        name="sb_sample",
    )(page_table, bias_rows, q_rows, kt_new, vt_new, lt, *([cache_kt] * npg), *([cache_vt] * npg))
    per_head = rows // N_KV_HEADS
    x = acc_t.reshape(db, N_KV_HEADS, HEAD_DIM, N_KV_HEADS, per_head)
    heads = jnp.arange(N_KV_HEADS)
    return x[:, heads, :, heads, :].transpose(1, 0, 3, 2).reshape(db, rows, HEAD_DIM).astype(BF16)


def _ln_silu(c, g, b):
    mu = jnp.mean(c, axis=-1, keepdims=True)
    cc = c - mu
    var = jnp.mean(cc * cc, axis=-1, keepdims=True)
    y = cc * lax.rsqrt(var + LN_EPS) * g + b
    return y * (1.0 / (1.0 + jnp.exp(-y)))


def _conv_prompt_kernel(prev_ref, cur_ref, w_ref, b_ref, g_ref, be_ref, o_ref, ext_ref, *, tiles_per_seq):
    i = pl.program_id(0)
    tm = cur_ref.shape[0]
    halo = prev_ref.shape[0]
    first = (i % tiles_per_seq) == 0
    ext_ref[0:halo, :] = jnp.where(first, 0.0, prev_ref[...])
    ext_ref[halo:halo + tm, :] = cur_ref[...]
    base = halo - (CONV_WIDTH - 1)
    acc = jnp.zeros((tm, cur_ref.shape[1]), F32) + b_ref[...]
    for w in range(CONV_WIDTH):
        acc = acc + ext_ref[base + w:base + w + tm, :] * w_ref[w:w + 1, :]
    o_ref[...] = _ln_silu(acc, g_ref[...], be_ref[...]).astype(BF16)


def _conv_prompt(u, w_dw, b_dw, ln_g, ln_b, n_prompt, seq):
    d = u.shape[1]
    tm, halo = CONV_TILE, CONV_HALO
    ratio = tm // halo
    fixed = lambda i: (0, 0)
    vec = lambda a: a.reshape(1, d)
    return pl.pallas_call(
        functools.partial(_conv_prompt_kernel, tiles_per_seq=seq // tm),
        grid=(n_prompt // tm,),
        in_specs=[pl.BlockSpec((halo, d), lambda i: (jnp.maximum(i * ratio - 1, 0), 0)),
                  pl.BlockSpec((tm, d), lambda i: (i, 0)),
                  pl.BlockSpec((CONV_WIDTH, d), fixed),
                  pl.BlockSpec((1, d), fixed), pl.BlockSpec((1, d), fixed), pl.BlockSpec((1, d), fixed)],
        out_specs=pl.BlockSpec((tm, d), lambda i: (i, 0)),
        out_shape=jax.ShapeDtypeStruct((n_prompt, d), BF16),
        scratch_shapes=[pltpu.VMEM((halo + tm, d), F32)],
        compiler_params=_cparams("parallel"),
        name="conv_prompt",
    )(u, u, w_dw, vec(b_dw), vec(ln_g), vec(ln_b))


def _conv_sample_kernel(st_ref, u_ref, wst_ref, wu_ref, b_ref, g_ref, be_ref, o_ref):
    st, u = st_ref[...], u_ref[...]
    t_new = u.shape[1]
    for t in range(t_new):
        c = (jnp.sum(st * wst_ref[t][None], axis=1) + jnp.sum(u * wu_ref[t][None], axis=1) + b_ref[...])
        o_ref[t] = _ln_silu(c, g_ref[...], be_ref[...]).astype(BF16)


def _conv_sample(state, u_s, w_dw, b_dw, ln_g, ln_b):
    db, n_left, d = state.shape
    t_new = u_s.shape[1]
    bb = CONV_SAMPLE_BATCH
    r = jnp.arange(n_left)[None, :] - jnp.arange(t_new)[:, None]
    wst = jnp.where((r >= 0)[..., None], w_dw[jnp.clip(r, 0, CONV_WIDTH - 1)], 0.0)
    s = n_left - jnp.arange(t_new)[:, None] + jnp.arange(t_new)[None, :]
    wu = jnp.where((s <= CONV_WIDTH - 1)[..., None], w_dw[jnp.clip(s, 0, CONV_WIDTH - 1)], 0.0)
    fixed2 = lambda i: (0, 0)
    fixed3 = lambda i: (0, 0, 0)
    vec = lambda a: a.reshape(1, d)
    out = pl.pallas_call(
        _conv_sample_kernel,
        grid=(db // bb,),
        in_specs=[pl.BlockSpec((bb, n_left, d), lambda i: (i, 0, 0)),
                  pl.BlockSpec((bb, t_new, d), lambda i: (i, 0, 0)),
                  pl.BlockSpec((t_new, n_left, d), fixed3), pl.BlockSpec((t_new, t_new, d), fixed3),
                  pl.BlockSpec((1, d), fixed2), pl.BlockSpec((1, d), fixed2), pl.BlockSpec((1, d), fixed2)],
        out_specs=pl.BlockSpec((t_new, bb, d), lambda i: (0, i, 0)),
        out_shape=jax.ShapeDtypeStruct((t_new, db, d), BF16),
        compiler_params=_cparams("parallel"),
        name="conv_sample",
    )(state, u_s, wst, wu, vec(b_dw), vec(ln_g), vec(ln_b))
    return out.transpose(1, 0, 2).reshape(db * t_new, d)


def _swa_prompt_kernel(sink_ref, q_ref, kp_ref, kc_ref, vp_ref, vc_ref, o_ref):
    i = pl.program_id(1)
    w = q_ref.shape[0]
    q = q_ref[...]
    k2 = jnp.concatenate([kp_ref[...], kc_ref[...]], axis=0)
    v2 = jnp.concatenate([vp_ref[...], vc_ref[...]], axis=0)
    rows = GROUP * w
    r = lax.broadcasted_iota(jnp.int32, (rows, 2 * w), 0)
    c = lax.broadcasted_iota(jnp.int32, (rows, 2 * w), 1)
    rel = (r & (w - 1)) - c + w
    first_valid = jnp.where(i > 0, 0, w)
    mask = (rel >= 0) & (rel <= WINDOW) & (c >= first_valid)
    outs = []
    for h in range(N_KV_HEADS):
        qh = jnp.concatenate(
            [q[:, (h * GROUP + g) * HEAD_DIM:(h * GROUP + g + 1) * HEAD_DIM] for g in range(GROUP)], axis=0)
        z = _dot_nt(qh, k2[:, h * HEAD_DIM:(h + 1) * HEAD_DIM])
        z = jnp.where(mask, z, -jnp.inf)
        s = jnp.concatenate([jnp.full((w, 1), sink_ref[h * GROUP + g], F32) for g in range(GROUP)], axis=0)
        m = jnp.maximum(jnp.max(z, axis=-1, keepdims=True), s)
        e = jnp.exp(z - m)
        den = jnp.sum(e, axis=-1, keepdims=True) + jnp.exp(s - m)
        o = _dot(e.astype(BF16), v2[:, h * HEAD_DIM:(h + 1) * HEAD_DIM]) / den
        outs += [o[g * w:(g + 1) * w] for g in range(GROUP)]
    o_ref[...] = jnp.concatenate(outs, axis=1).astype(BF16)


def _swa_prompt(sinks, q, k, v, batch, seq):
    w = WINDOW
    nb = seq // w
    cur = lambda b, i: (b * nb + i, 0)
    prev = lambda b, i: (b * nb + jnp.maximum(i - 1, 0), 0)
    return pl.pallas_call(
        _swa_prompt_kernel,
        grid=(batch, nb),
        in_specs=[pl.BlockSpec(memory_space=pltpu.SMEM),
                  pl.BlockSpec((w, Q_DIM), cur),
                  pl.BlockSpec((w, KV_DIM), prev), pl.BlockSpec((w, KV_DIM), cur),
                  pl.BlockSpec((w, KV_DIM), prev), pl.BlockSpec((w, KV_DIM), cur)],
        out_specs=pl.BlockSpec((w, Q_DIM), cur),
        out_shape=jax.ShapeDtypeStruct((batch * seq, Q_DIM), BF16),
        compiler_params=_cparams("parallel", "parallel"),
        name="swa_prompt",
    )(sinks, q, k, k, v, v)


def _swa_sample_kernel(sink_ref, q_ref, kb_ref, vb_ref, kn_ref, vn_ref, o_ref, *, t_new):
    qbd = _block_diag_q(q_ref[0])
    rows = qbd.shape[0]
    wb = kb_ref.shape[1]
    kn = _pad_rows(kn_ref[0], wb).astype(BF16)
    vn = _pad_rows(vn_ref[0], wb).astype(BF16)
    r = lax.broadcasted_iota(jnp.int32, (rows, wb), 0)
    c = lax.broadcasted_iota(jnp.int32, (rows, wb), 1)
    t = _div_pow2(r, GROUP) & (t_new - 1)
    z1 = jnp.where((t + wb - c >= 0) & (t + wb - c <= WINDOW), _dot_nt(qbd, kb_ref[0].astype(BF16)), -jnp.inf)
    z2 = jnp.where((t - c >= 0) & (c < t_new), _dot_nt(qbd, kn), -jnp.inf)
    s = sink_ref[:, 0:1]
    m = jnp.maximum(jnp.maximum(jnp.max(z1, axis=-1, keepdims=True), jnp.max(z2, axis=-1, keepdims=True)), s)
    e1, e2 = jnp.exp(z1 - m), jnp.exp(z2 - m)
    den = jnp.sum(e1, axis=-1, keepdims=True) + jnp.sum(e2, axis=-1, keepdims=True) + jnp.exp(s - m)
    o = (_dot(e1.astype(BF16), vb_ref[0].astype(BF16)) + _dot(e2.astype(BF16), vn)) / den
    o_ref[0] = _diag_blocks(o).astype(BF16)


def _swa_sample(sink_rows, q_rows, k_buf, v_buf, k_new, v_new):
    db, wb, _ = k_buf.shape
    rows = q_rows.shape[1]
    t_pad = k_new.shape[1]
    per_b = lambda b: (b, 0, 0)
    return pl.pallas_call(
        functools.partial(_swa_sample_kernel, t_new=rows // N_Q_HEADS),
        grid=(db,),
        in_specs=[pl.BlockSpec((rows, LANES), lambda b: (0, 0)),
                  pl.BlockSpec((1, rows, HEAD_DIM), per_b),
                  pl.BlockSpec((1, wb, KV_DIM), per_b), pl.BlockSpec((1, wb, KV_DIM), per_b),
                  pl.BlockSpec((1, t_pad, KV_DIM), per_b), pl.BlockSpec((1, t_pad, KV_DIM), per_b)],
        out_specs=pl.BlockSpec((1, rows, HEAD_DIM), per_b),
        out_shape=jax.ShapeDtypeStruct((db, rows, HEAD_DIM), BF16),
        compiler_params=_cparams("parallel"),
        name="swa_sample",
    )(sink_rows, q_rows, k_buf, v_buf, k_new, v_new)


def _sample_q_rows(q_s, db, t):
    return q_s.reshape(db, t, N_KV_HEADS, GROUP, HEAD_DIM).transpose(0, 2, 1, 3, 4).reshape(
        db, N_Q_HEADS * t, HEAD_DIM)


def _sample_o_tokens(o_rows, db, t):
    return o_rows.reshape(db, N_KV_HEADS, t, GROUP, HEAD_DIM).transpose(0, 2, 1, 3, 4).reshape(db * t, Q_DIM)


def _head_rows(per_head, t):
    rows = jnp.broadcast_to(per_head.reshape(N_KV_HEADS, 1, GROUP), (N_KV_HEADS, t, GROUP)).reshape(-1, 1)
    return jnp.broadcast_to(rows, (rows.shape[0], LANES)).astype(F32)


def _pad_new(x, db, t):
    return jnp.pad(x.reshape(db, t, KV_DIM), ((0, 0), (0, 8 - t), (0, 0)))


def _new_keys_on_lanes(x, db, t, width):
    xt = x.reshape(db, t, KV_DIM).transpose(0, 2, 1).astype(BF16)
    return jnp.pad(xt, ((0, 0), (0, 0), (0, width - t)))


def _rope_tables(pos):
    half = ROPE_DIM // 2
    inv = ROPE_THETA ** (-jnp.arange(half, dtype=F32) / half)
    ang = pos.astype(F32)[:, None] * inv[None, :]
    cos, sin = jnp.cos(ang), jnp.sin(ang)
    n = pos.shape[0]
    rest = HEAD_DIM - ROPE_DIM
    c = jnp.concatenate([cos, cos, jnp.ones((n, rest), F32)], axis=1)
    sa = jnp.concatenate([-sin, jnp.zeros((n, half + rest), F32)], axis=1)
    sb = jnp.concatenate([jnp.zeros((n, half), F32), sin, jnp.zeros((n, rest), F32)], axis=1)
    rep = LANES // HEAD_DIM
    return tuple(jnp.concatenate([x] * rep, axis=1) for x in (c, sa, sb))


def kernel(x_prompt, x_sample, cache_sb_k, cache_sb_v, state_conv, cache_swa_k, cache_swa_v, page_table,
           norm_mix, norm_ffn, w_ffn_up, w_ffn_down, w_sb_qkv, w_sb_o, sb_bias,
           w_cv_pw1, b_cv_pw1, w_cv_dw, b_cv_dw, cv_ln_g, cv_ln_b, w_cv_pw2, b_cv_pw2,
           w_swa_qkv, b_swa_qkv, swa_sinks, w_swa_o, b_swa_o, norm_final):
    batch, seq, d = x_prompt.shape
    db, t_new, _ = x_sample.shape
    depth = norm_mix.shape[0]
    past_len = page_table.shape[1] * cache_sb_k.shape[2]
    n_p, n_s = batch * seq, db * t_new
    assert (n_p + n_s) % TOKEN_TILE == 0 and n_p % TOKEN_TILE == 0
    assert cache_swa_k.shape[2] == WINDOW and t_new <= 8

    x = jnp.concatenate([x_prompt.reshape(n_p, d), x_sample.reshape(n_s, d)], axis=0)
    n_phys, page = cache_sb_k.shape[1], cache_sb_k.shape[2]
    cache_kt = cache_sb_k.transpose(0, 1, 3, 4, 2).reshape(cache_sb_k.shape[0], n_phys, KV_DIM, page)
    cache_vt = cache_sb_v.transpose(0, 1, 3, 4, 2).reshape(cache_sb_v.shape[0], n_phys, KV_DIM, page)
    jj = jnp.arange(SB_K_BLOCK)
    lt = (jj[:, None] >= jj[None, :]).astype(BF16)
    zero_bias_d = jnp.zeros((d,), F32)

    outs = {k: [] for k in ("sbkp", "sbvp", "sbks", "sbvs", "cvp", "cvs", "swkp", "swvp", "swks", "swvs")}
    a_i = c_i = w_i = 0
    for layer in range(depth):
        kind = layer % N_MIXERS
        if kind == 0:
            w_qkv = w_sb_qkv[a_i].astype(BF16)
            no_bias = jnp.zeros((Q_DIM + 2 * KV_DIM,), F32)
            bias2 = sb_bias[a_i] * LOG2E
            tiles_per_seq = seq // TOKEN_TILE
            blocks_per_tile = TOKEN_TILE // SB_K_BLOCK
            kvt_f32 = (jax.ShapeDtypeStruct((batch, KV_DIM, seq), F32),
                       pl.BlockSpec((1, KV_DIM, TOKEN_TILE), lambda i: (i // tiles_per_seq, 0, i % tiles_per_seq)))
            kvt_blk = (jax.ShapeDtypeStruct((n_p // SB_K_BLOCK, KV_DIM, SB_K_BLOCK), BF16),
                       pl.BlockSpec((blocks_per_tile, KV_DIM, SB_K_BLOCK), lambda i: (i, 0, 0)))
            q_p, ktf, vtf, ktb, vtb = _norm_linear(
                x, norm_mix[layer], w_qkv, no_bias, _sb_qkv_prompt_epilogue,
                [_row_out(n_p, Q_DIM, BF16), kvt_f32, kvt_f32, kvt_blk, kvt_blk],
                n_tiles=n_p // TOKEN_TILE, name="sb_qkv_prompt")
            q_s, k_s, v_s = _norm_linear(
                x, norm_mix[layer], w_qkv, no_bias, _sb_qkv_epilogue,
                [_row_out(n_s, Q_DIM, BF16), _row_out(n_s, KV_DIM, F32), _row_out(n_s, KV_DIM, F32)],
                first_tile=n_p // TOKEN_TILE, name="sb_qkv_sample")
            o_p = _sb_prompt(bias2, q_p, ktb, vtb, lt, batch, seq)
            o_s = _sb_sample(page_table, _head_rows(bias2, t_new), _sample_q_rows(q_s, db, t_new),
                             _new_keys_on_lanes(k_s, db, t_new, page), _new_keys_on_lanes(v_s, db, t_new, page),
                             lt, cache_kt, cache_vt, a_i)
            o_s = _sample_o_tokens(o_s, db, t_new)
            w_o, b_o = w_sb_o[a_i], zero_bias_d
            heads_t = lambda t: t.reshape(batch, N_KV_HEADS, HEAD_DIM, seq).transpose(0, 3, 1, 2)
            outs["sbkp"].append(heads_t(ktf))
            outs["sbvp"].append(heads_t(vtf))
            outs["sbks"].append(k_s.reshape(db, t_new, N_KV_HEADS, HEAD_DIM))
            outs["sbvs"].append(v_s.reshape(db, t_new, N_KV_HEADS, HEAD_DIM))
            a_i += 1
        elif kind == 1:
            (u,) = _norm_linear(x, norm_mix[layer], w_cv_pw1[c_i].astype(BF16), b_cv_pw1[c_i],
                                _glu_epilogue, [_row_out(n_p + n_s, d, F32)], name="conv_pw1")
            u_s = u[n_p:].reshape(db, t_new, d)
            o_p = _conv_prompt(u, w_cv_dw[c_i], b_cv_dw[c_i], cv_ln_g[c_i], cv_ln_b[c_i], n_p, seq)
            o_s = _conv_sample(state_conv[c_i], u_s, w_cv_dw[c_i], b_cv_dw[c_i], cv_ln_g[c_i], cv_ln_b[c_i])
            w_o, b_o = w_cv_pw2[c_i], b_cv_pw2[c_i]
            n_left = CONV_WIDTH - 1
            outs["cvp"].append(jnp.stack([u[(b + 1) * seq - n_left:(b + 1) * seq] for b in range(batch)]))
            outs["cvs"].append(jnp.concatenate([state_conv[c_i], u_s], axis=1)[:, -n_left:])
            c_i += 1
        else:
            pos = jnp.concatenate([jnp.tile(jnp.arange(seq, dtype=jnp.int32), batch),
                                   jnp.tile(past_len + jnp.arange(t_new, dtype=jnp.int32), db)])
            q, k, v = _norm_linear(
                x, norm_mix[layer], w_swa_qkv[w_i].astype(BF16), b_swa_qkv[w_i],
                _swa_qkv_epilogue,
                [_row_out(n_p + n_s, Q_DIM, BF16), _row_out(n_p + n_s, KV_DIM, F32),
                 _row_out(n_p + n_s, KV_DIM, F32)],
                extra=_rope_tables(pos), name="swa_qkv")
            kp, vp = k[:n_p], v[:n_p]
            o_p = _swa_prompt(swa_sinks[w_i], q, kp.astype(BF16), vp.astype(BF16), batch, seq)
            k_new, v_new = k[n_p:].reshape(db, t_new, KV_DIM), v[n_p:].reshape(db, t_new, KV_DIM)
            k_buf = cache_swa_k[w_i].reshape(db, WINDOW, KV_DIM)
            v_buf = cache_swa_v[w_i].reshape(db, WINDOW, KV_DIM)
            o_s = _swa_sample(_head_rows(swa_sinks[w_i], t_new), _sample_q_rows(q[n_p:], db, t_new),
                              k_buf, v_buf, _pad_new(k[n_p:], db, t_new), _pad_new(v[n_p:], db, t_new))
            o_s = _sample_o_tokens(o_s, db, t_new)
            w_o, b_o = w_swa_o[w_i], b_swa_o[w_i]
            heads = (N_KV_HEADS, HEAD_DIM)
            outs["swkp"].append(kp.reshape(batch, seq, *heads)[:, seq - WINDOW:])
            outs["swvp"].append(vp.reshape(batch, seq, *heads)[:, seq - WINDOW:])
            outs["swks"].append(jnp.concatenate([k_buf, k_new], axis=1)[:, -WINDOW:].reshape(db, WINDOW, *heads))
            outs["swvs"].append(jnp.concatenate([v_buf, v_new], axis=1)[:, -WINDOW:].reshape(db, WINDOW, *heads))
            w_i += 1
        x = _proj_mlp(o_p, o_s, w_o.astype(BF16), b_o, x, norm_ffn[layer], w_ffn_up[layer].astype(BF16),
                      w_ffn_down[layer].astype(BF16), norm_final, final_norm=(layer == depth - 1))

    y_prompt, y_sample = x
    return (y_prompt.reshape(batch, seq, d), y_sample.reshape(db, t_new, d),
            jnp.stack(outs["sbkp"]), jnp.stack(outs["sbvp"]), jnp.stack(outs["sbks"]), jnp.stack(outs["sbvs"]),
            jnp.stack(outs["cvp"]), jnp.stack(outs["cvs"]),
            jnp.stack(outs["swkp"]), jnp.stack(outs["swvp"]), jnp.stack(outs["swks"]), jnp.stack(outs["swvs"]))
```

```python
import functools

import jax
import jax.numpy as jnp
from jax import lax
from jax.experimental import pallas as pl
from jax.experimental.pallas import tpu as pltpu

F32 = jnp.float32
BF16 = jnp.bfloat16

HEAD_DIM = 64
N_Q_HEADS = 16
N_KV_HEADS = 4
GROUP = N_Q_HEADS // N_KV_HEADS
Q_DIM = N_Q_HEADS * HEAD_DIM
KV_DIM = N_KV_HEADS * HEAD_DIM
N_MIXERS = 3
CONV_WIDTH = 31
WINDOW = 128
ROPE_THETA = 500000.0
ROPE_DIM = HEAD_DIM // 4
RMS_EPS = 1e-6
LN_EPS = 1e-5
QK_SCALE = HEAD_DIM ** -0.5
LOG2E = 1.4426950408889634
MASKED_LOGIT = -1e30

LANES = 128
SUBLANES = 8
VMEM_LIMIT = 56 * 1024 * 1024

TOKEN_TILE = 512
FF_TILE = 1024
SB_Q_TILE = 128
SB_K_BLOCK = 256
SB_BLOCKS_PER_TRIP = 1
SB_LAG = 3
SB_PAGES_PER_STEP = 64
CONV_TILE = 256
CONV_HALO = 32
CONV_SAMPLE_BATCH = 8


def _cparams(*sem, flags=None):
    return pltpu.CompilerParams(dimension_semantics=sem, vmem_limit_bytes=VMEM_LIMIT, flags=flags)


def _rms(x, g):
    ms = jnp.mean(x * x, axis=-1, keepdims=True)
    return x * lax.rsqrt(ms + RMS_EPS) * g


def _softplus2(y):
    return jnp.maximum(y, 0.0) + jnp.log(1.0 + jnp.exp2(-jnp.abs(y))) * LOG2E


def _div_pow2(x, n):
    assert n & (n - 1) == 0
    return x >> (n.bit_length() - 1)


def _dot(a, b):
    return jnp.dot(a, b, preferred_element_type=F32)


def _dot_nt(a, b):
    return lax.dot_general(a, b, (((1,), (1,)), ((), ())), preferred_element_type=F32)


def _sb_qkv_epilogue(acc, extra, outs):
    q_ref, k_ref, v_ref = outs
    q_ref[...] = (acc[:, :Q_DIM] * (QK_SCALE * LOG2E)).astype(BF16)
    k_ref[...] = acc[:, Q_DIM:Q_DIM + KV_DIM]
    v_ref[...] = acc[:, Q_DIM + KV_DIM:]


def _sb_qkv_prompt_epilogue(acc, extra, outs):
    q_ref, ktf_ref, vtf_ref, ktb_ref, vtb_ref = outs
    q_ref[...] = (acc[:, :Q_DIM] * (QK_SCALE * LOG2E)).astype(BF16)
    bk = ktb_ref.shape[2]
    for lo, f_ref, b_ref in ((Q_DIM, ktf_ref, ktb_ref), (Q_DIM + KV_DIM, vtf_ref, vtb_ref)):
        t = acc[:, lo:lo + KV_DIM].T
        f_ref[0] = t
        for blk in range(b_ref.shape[0]):
            b_ref[blk] = t[:, blk * bk:(blk + 1) * bk].astype(BF16)


def _rope_group(x, c, sa, sb):
    return x * c + pltpu.roll(x, LANES - ROPE_DIM // 2, 1) * sa + pltpu.roll(x, ROPE_DIM // 2, 1) * sb


def _swa_qkv_epilogue(acc, extra, outs):
    c_ref, sa_ref, sb_ref = extra
    q_ref, k_ref, v_ref = outs
    c, sa, sb = c_ref[...], sa_ref[...], sb_ref[...]
    for j in range(Q_DIM // LANES):
        x = acc[:, j * LANES:(j + 1) * LANES]
        q_ref[:, j * LANES:(j + 1) * LANES] = (_rope_group(x, c, sa, sb) * QK_SCALE).astype(BF16)
    for j in range(KV_DIM // LANES):
        x = acc[:, Q_DIM + j * LANES:Q_DIM + (j + 1) * LANES]
        k_ref[:, j * LANES:(j + 1) * LANES] = _rope_group(x, c, sa, sb)
    v_ref[...] = acc[:, Q_DIM + KV_DIM:]


def _glu_epilogue(acc, extra, outs):
    (u_ref,) = outs
    d = acc.shape[1] // 2
    u_ref[...] = acc[:, :d] * (1.0 / (1.0 + jnp.exp(-acc[:, d:])))


def _norm_linear_kernel(x_ref, g_ref, w_ref, b_ref, *refs, n_extra, epilogue):
    h = _rms(x_ref[...], g_ref[...]).astype(BF16)
    acc = _dot(h, w_ref[...]) + b_ref[...]
    epilogue(acc, refs[:n_extra], refs[n_extra:])


def _row_out(n, cols, dtype):
    return (jax.ShapeDtypeStruct((n, cols), dtype), pl.BlockSpec((TOKEN_TILE, cols), lambda i: (i, 0)))


def _norm_linear(x, g, w, b, epilogue, outs, extra=(), first_tile=0, n_tiles=None, name="norm_linear"):
    n, d = x.shape
    n_out = w.shape[1]
    tm = TOKEN_TILE
    n_tiles = n // tm - first_tile if n_tiles is None else n_tiles
    row = lambda i: (i + first_tile, 0)
    fixed = lambda i: (0, 0)
    in_specs = [pl.BlockSpec((tm, d), row), pl.BlockSpec((1, d), fixed),
                pl.BlockSpec((d, n_out), fixed), pl.BlockSpec((1, n_out), fixed)]
    in_specs += [pl.BlockSpec((tm, e.shape[1]), row) for e in extra]
    return pl.pallas_call(
        functools.partial(_norm_linear_kernel, n_extra=len(extra), epilogue=epilogue),
        grid=(n_tiles,),
        in_specs=in_specs,
        out_specs=[spec for _, spec in outs],
        out_shape=[shape for shape, _ in outs],
        compiler_params=_cparams("parallel"),
        name=name,
    )(x, g.reshape(1, d), w, b.reshape(1, n_out), *extra)


def _proj_mlp_kernel(op_ref, os_ref, wo_ref, bo_ref, x_ref, g_ref, wu_ref, wd_ref, gf_ref, *refs,
                     prompt_tiles, final_norm):
    out_refs, (acc_ref, h_ref) = refs[:-2], refs[-2:]
    i = pl.program_id(0)
    k = pl.program_id(1)
    is_prompt = i < prompt_tiles

    def start(o_ref):
        x1 = x_ref[...] + _dot(o_ref[...], wo_ref[...]) + bo_ref[...]
        acc_ref[...] = x1
        h_ref[...] = _rms(x1, g_ref[...]).astype(BF16)

    @pl.when(jnp.logical_and(k == 0, is_prompt))
    def _():
        start(op_ref)

    @pl.when(jnp.logical_and(k == 0, jnp.logical_not(is_prompt)))
    def _():
        start(os_ref)

    u = jnp.maximum(_dot(h_ref[...], wu_ref[...]), 0.0)
    acc_ref[...] += _dot((u * u).astype(BF16), wd_ref[...])
    last = k == pl.num_programs(1) - 1
    if final_norm:
        yp_ref, ys_ref = out_refs

        @pl.when(jnp.logical_and(last, is_prompt))
        def _():
            yp_ref[...] = _rms(acc_ref[...], gf_ref[...])

        @pl.when(jnp.logical_and(last, jnp.logical_not(is_prompt)))
        def _():
            ys_ref[...] = _rms(acc_ref[...], gf_ref[...])
    else:
        @pl.when(last)
        def _():
            out_refs[0][...] = acc_ref[...]


def _proj_mlp(o_p, o_s, wo, bo, x, g, wu, wd, gf, final_norm):
    n, d = x.shape
    n_p, n_s = o_p.shape[0], o_s.shape[0]
    d_ff = wu.shape[1]
    tm, tf = TOKEN_TILE, FF_TILE
    prompt_tiles = n_p // tm
    row = lambda i, k: (i, 0)
    fixed = lambda i, k: (0, 0)
    prompt_row = lambda i, k: (jnp.minimum(i, prompt_tiles - 1), 0)
    sample_row = lambda i, k: (jnp.maximum(i - prompt_tiles, 0), 0)
    if final_norm:
        out_specs = [pl.BlockSpec((tm, d), prompt_row), pl.BlockSpec((tm, d), sample_row)]
        out_shape = [jax.ShapeDtypeStruct((n_p, d), F32), jax.ShapeDtypeStruct((n_s, d), F32)]
    else:
        out_specs = pl.BlockSpec((tm, d), row)
        out_shape = jax.ShapeDtypeStruct((n, d), F32)
    return pl.pallas_call(
        functools.partial(_proj_mlp_kernel, prompt_tiles=prompt_tiles, final_norm=final_norm),
        grid=(n // tm, d_ff // tf),
        in_specs=[pl.BlockSpec((tm, d), prompt_row), pl.BlockSpec((tm, d), sample_row),
                  pl.BlockSpec((d, d), fixed), pl.BlockSpec((1, d), fixed),
                  pl.BlockSpec((tm, d), row), pl.BlockSpec((1, d), fixed),
                  pl.BlockSpec((d, tf), lambda i, k: (0, k)), pl.BlockSpec((tf, d), lambda i, k: (k, 0)),
                  pl.BlockSpec((1, d), fixed)],
        out_specs=out_specs,
        out_shape=out_shape,
        scratch_shapes=[pltpu.VMEM((tm, d), F32), pltpu.VMEM((tm, d), BF16)],
        compiler_params=_cparams("parallel", "arbitrary"),
        name="proj_mlp",
    )(o_p, o_s, wo, bo.reshape(1, d), x, g.reshape(1, d), wu, wd, gf.reshape(1, d))


def _sb_prompt_kernel(bias_ref, q_ref, kt_ref, vt_ref, lt_ref, o_ref,
                      raw_scr, y_scr, tail_scr, a_scr, carry_scr, acc_scr, *, tq, bk):
    kv = pl.program_id(1)
    rows = GROUP * tq
    nh = y_scr.shape[1] // bk
    wide = nh * bk
    nq = q_ref.shape[0] // tq
    cols = lambda h: slice(h * bk, (h + 1) * bk)
    diag_group = lambda i: (i * tq) // wide
    n_trips = lambda i: jnp.maximum(diag_group(i) + 1, SB_LAG)

    def load_q(i):
        q = q_ref[pl.ds(pl.multiple_of(i * tq, tq), tq), :]
        return jnp.concatenate([q[:, j * HEAD_DIM:(j + 1) * HEAD_DIM] for j in range(GROUP)], axis=0)

    def trip(q_all, score_group, logits, weights, value_group):
        raws = av = None
        if q_all is not None:
            raws = [_dot(q_all, kt_ref[nh * score_group + h]) for h in range(nh)]
        if value_group is not None:
            av = _dot_nt(a_scr[:, cols(0)], vt_ref[nh * value_group])
            for h in range(1, nh):
                av = av + _dot_nt(a_scr[:, cols(h)], vt_ref[nh * value_group + h])
        if weights:
            c = carry_scr[...]
            for h in reversed(range(nh)):
                tail = tail_scr[:, cols(h)]
                w = y_scr[:, cols(h)] - tail - jnp.concatenate([c] * (bk // LANES), axis=1)
                a_scr[:, cols(h)] = jnp.exp2(w).astype(BF16)
                c = c + jnp.broadcast_to(tail[:, 0:1], c.shape)
            carry_scr[...] = c
        if logits is not None:
            for h in range(nh):
                y = raw_scr[:, cols(h)]
                y = jnp.concatenate(
                    [y[j * tq:(j + 1) * tq] + bias_ref[kv * GROUP + j] for j in range(GROUP)], axis=0)
                if logits[0] == "diag":
                    _, tile, group = logits
                    kpos = (nh * group + h) * bk + lax.broadcasted_iota(jnp.int32, (rows, bk), 1)
                    qpos = tile * tq + (lax.broadcasted_iota(jnp.int32, (rows, bk), 0) & (tq - 1))
                    y = jnp.where(kpos < qpos, y, MASKED_LOGIT)
                elif logits[0] == "pad":
                    y = jnp.where(logits[1], MASKED_LOGIT, y)
                y_scr[:, cols(h)] = y
                sp = _softplus2(y).astype(BF16)
                tail_scr[:, cols(h)] = _dot(sp, lt_ref[...])
        if raws is not None:
            for h in range(nh):
                raw_scr[:, cols(h)] = raws[h]
        if av is not None:
            acc_scr[...] += av

    def emit(tile):
        acc = acc_scr[...]
        start = tile * tq if isinstance(tile, int) else pl.multiple_of(tile * tq, tq)
        o_ref[pl.ds(start, tq), :] = jnp.concatenate(
            [acc[j * tq:(j + 1) * tq] for j in range(GROUP)], axis=1).astype(BF16)

    raw_scr[...] = jnp.zeros_like(raw_scr)
    y_scr[...] = jnp.full_like(y_scr, MASKED_LOGIT)
    tail_scr[...] = jnp.zeros_like(tail_scr)
    a_scr[...] = jnp.zeros_like(a_scr)
    carry_scr[...] = jnp.zeros_like(carry_scr)
    acc_scr[...] = jnp.zeros_like(acc_scr)

    def tile_trips(i, _):
        q_all = load_q(i)
        jd = diag_group(i)
        prev = jnp.maximum(i - 1, 0)
        jd_prev, trips_prev = diag_group(prev), n_trips(prev)
        group = lambda n: jnp.maximum(jd - n, 0)
        prev_group = lambda n: jnp.maximum(jd_prev - (trips_prev - SB_LAG + n), 0)
        prev_padded = jnp.logical_or(i == 0, jd_prev + 1 < SB_LAG)
        trip(q_all, group(0), ("pad", prev_padded), True, prev_group(0))
        trip(q_all, group(1), ("diag", i, jd), True, prev_group(1))
        carry_scr[...] = jnp.zeros_like(carry_scr)
        trip(q_all, group(2), ("pad", jd < 1), True, prev_group(2))

        @pl.when(i > 0)
        def _():
            emit(i - 1)

        acc_scr[...] = jnp.zeros_like(acc_scr)

        def steady(n, _):
            trip(q_all, jd - n, ("plain",), True, jd - (n - SB_LAG))
            return 0

        lax.fori_loop(SB_LAG, n_trips(i), steady, 0)
        return 0

    lax.fori_loop(0, nq, tile_trips, 0)

    last = nq - 1
    jd_last = (last * tq) // wide
    trips_last = max(jd_last + 1, SB_LAG)
    last_group = lambda n: max(jd_last - (trips_last - SB_LAG + n), 0)
    trip(None, None, ("pad", True) if jd_last + 1 < SB_LAG else ("plain",), True, last_group(0))
    trip(None, None, None, True, last_group(1))
    trip(None, None, None, False, last_group(2))
    emit(last)


def _sb_prompt(bias2, q, ktb, vtb, lt, batch, seq):
    tq, bk = SB_Q_TILE, SB_K_BLOCK
    nb = seq // bk
    rows = GROUP * tq
    wide = SB_BLOCKS_PER_TRIP * bk
    assert seq % wide == 0 and seq % tq == 0
    kv_spec = pl.BlockSpec((nb, HEAD_DIM, bk), lambda b, h: (b, h, 0))
    q_spec = pl.BlockSpec((seq, GROUP * HEAD_DIM), lambda b, h: (b, h))
    return pl.pallas_call(
        functools.partial(_sb_prompt_kernel, tq=tq, bk=bk),
        grid=(batch, N_KV_HEADS),
        in_specs=[pl.BlockSpec(memory_space=pltpu.SMEM), q_spec, kv_spec, kv_spec,
                  pl.BlockSpec((bk, bk), lambda b, h: (0, 0))],
        out_specs=q_spec,
        out_shape=jax.ShapeDtypeStruct((batch * seq, Q_DIM), BF16),
        scratch_shapes=[pltpu.VMEM((rows, wide), F32), pltpu.VMEM((rows, wide), F32),
                        pltpu.VMEM((rows, wide), F32), pltpu.VMEM((rows, wide), BF16),
                        pltpu.VMEM((rows, LANES), F32), pltpu.VMEM((rows, HEAD_DIM), F32)],
        compiler_params=_cparams("parallel", "arbitrary"),
        name="sb_prompt",
    )(bias2, q, ktb, vtb, lt)


def _block_diag_q(q):
    q4 = jnp.concatenate([q] * N_KV_HEADS, axis=1)
    r = lax.broadcasted_iota(jnp.int32, q4.shape, 0)
    c = lax.broadcasted_iota(jnp.int32, q4.shape, 1)
    rows_per_head = q.shape[0] // N_KV_HEADS
    return jnp.where(_div_pow2(r, rows_per_head) == _div_pow2(c, HEAD_DIM), q4, jnp.zeros_like(q4))


def _diag_blocks(o):
    rows_per_head = o.shape[0] // N_KV_HEADS
    return jnp.concatenate(
        [o[h * rows_per_head:(h + 1) * rows_per_head, h * HEAD_DIM:(h + 1) * HEAD_DIM]
         for h in range(N_KV_HEADS)], axis=0)


def _pad_rows(x, rows):
    return jnp.concatenate([x, jnp.zeros((rows - x.shape[0], x.shape[1]), x.dtype)], axis=0)


def _sb_sample_block(qbd, bias, kt, vt, lt, carry, acc_t, mask):
    y = _dot(qbd, kt) + bias
    sp = _softplus2(y)
    if mask is not None:
        sp = jnp.where(mask, sp, 0.0)
    tail = _dot(sp.astype(BF16), lt)
    a = jnp.exp2(y - tail - jnp.concatenate([carry] * (y.shape[1] // LANES), axis=1))
    if mask is not None:
        a = jnp.where(mask, a, 0.0)
    acc_t = acc_t + _dot_nt(vt, a.astype(BF16))
    carry = carry + jnp.broadcast_to(tail[:, 0:1], carry.shape)
    return carry, acc_t


def _sb_sample_kernel(pt_ref, bias_ref, q_ref, kn_ref, vn_ref, lt_ref, *refs, npg, t_new):
    k_refs, v_refs = refs[:npg], refs[npg:2 * npg]
    o_ref, carry_ref, acc_ref = refs[2 * npg:]
    j = pl.program_id(1)
    bk = lt_ref.shape[0]
    page = k_refs[0].shape[3]
    qbd = _block_diag_q(q_ref[0])
    rows = qbd.shape[0]
    bias1 = bias_ref[...]
    bias = jnp.concatenate([bias1] * (bk // LANES), axis=1)
    lt = lt_ref[...]

    @pl.when(j == 0)
    def _():
        r = lax.broadcasted_iota(jnp.int32, (rows, page), 0)
        c = lax.broadcasted_iota(jnp.int32, (rows, page), 1)
        t = _div_pow2(r, GROUP) & (t_new - 1)
        carry, acc_t = _sb_sample_block(
            qbd, bias1, kn_ref[0], vn_ref[0], lt[:page, :page],
            jnp.zeros((rows, LANES), F32), jnp.zeros((KV_DIM, rows), F32), c < t)
        carry_ref[...] = carry
        acc_ref[...] = acc_t

    carry, acc_t = carry_ref[...], acc_ref[...]
    pages_per_block = bk // page
    blocks = [range(b * pages_per_block, (b + 1) * pages_per_block)
              for b in reversed(range(npg // pages_per_block))]
    cat = lambda refs, sel: jnp.concatenate([refs[p][0, 0] for p in sel], axis=1).astype(BF16)
    ys = [_dot(qbd, cat(k_refs, sel)) + bias for sel in blocks]
    tails = [_dot(_softplus2(y).astype(BF16), lt) for y in ys]
    for y, tail, sel in zip(ys, tails, blocks):
        a = jnp.exp2(y - tail - jnp.concatenate([carry] * (bk // LANES), axis=1))
        acc_t = acc_t + _dot_nt(cat(v_refs, sel), a.astype(BF16))
        carry = carry + jnp.broadcast_to(tail[:, 0:1], carry.shape)
    carry_ref[...] = carry
    acc_ref[...] = acc_t

    @pl.when(j == pl.num_programs(1) - 1)
    def _():
        o_ref[0] = acc_t


def _sb_sample(page_table, bias_rows, q_rows, kt_new, vt_new, lt, cache_kt, cache_vt, layer):
    db, n_pages = page_table.shape
    npg = min(SB_PAGES_PER_STEP, n_pages)
    assert n_pages % npg == 0
    nj = n_pages // npg
    page_size = cache_kt.shape[3]
    rows = q_rows.shape[1]

    def page_spec(p):
        return pl.BlockSpec((1, 1, KV_DIM, page_size),
                            lambda b, j, pt: (layer, pt[b, (nj - 1 - j) * npg + p], 0, 0))

    per_b = lambda b, j, pt: (b, 0, 0)
    fixed = lambda b, j, pt: (0, 0)
    grid_spec = pltpu.PrefetchScalarGridSpec(
        num_scalar_prefetch=1,
        grid=(db, nj),
        in_specs=[pl.BlockSpec((rows, LANES), fixed),
                  pl.BlockSpec((1, rows, HEAD_DIM), per_b),
                  pl.BlockSpec((1, KV_DIM, page_size), per_b),
                  pl.BlockSpec((1, KV_DIM, page_size), per_b),
                  pl.BlockSpec(lt.shape, fixed)]
                 + [page_spec(p) for p in range(npg)] + [page_spec(p) for p in range(npg)],
        out_specs=pl.BlockSpec((1, KV_DIM, rows), per_b),
        scratch_shapes=[pltpu.VMEM((rows, LANES), F32), pltpu.VMEM((KV_DIM, rows), F32)],
    )
    acc_t = pl.pallas_call(
        functools.partial(_sb_sample_kernel, npg=npg, t_new=rows // N_Q_HEADS),
        grid_spec=grid_spec,
        out_shape=jax.ShapeDtypeStruct((db, KV_DIM, rows), F32),
        compiler_params=_cparams("parallel", "arbitrary"),
        name="sb_sample",
    )(page_table, bias_rows, q_rows, kt_new, vt_new, lt, *([cache_kt] * npg), *([cache_vt] * npg))
    per_head = rows // N_KV_HEADS
    x = acc_t.reshape(db, N_KV_HEADS, HEAD_DIM, N_KV_HEADS, per_head)
    heads = jnp.arange(N_KV_HEADS)
    return x[:, heads, :, heads, :].transpose(1, 0, 3, 2).reshape(db, rows, HEAD_DIM).astype(BF16)


def _ln_silu(c, g, b):
    mu = jnp.mean(c, axis=-1, keepdims=True)
    cc = c - mu
    var = jnp.mean(cc * cc, axis=-1, keepdims=True)
    y = cc * lax.rsqrt(var + LN_EPS) * g + b
    return y * (1.0 / (1.0 + jnp.exp(-y)))


def _conv_prompt_kernel(prev_ref, cur_ref, w_ref, b_ref, g_ref, be_ref, o_ref, ext_ref, shift_ref,
                        *, tiles_per_seq):
    i = pl.program_id(0)
    tm = cur_ref.shape[0]
    halo = prev_ref.shape[0]
    sub = shift_ref.shape[0]
    first = (i % tiles_per_seq) == 0
    ext_ref[0:halo, :] = jnp.where(first, 0.0, prev_ref[...])
    ext_ref[halo:halo + tm, :] = cur_ref[...]
    for s in range(1, sub):
        shift_ref[s] = ext_ref[s:s + halo + tm - sub, :]
    base = halo - (CONV_WIDTH - 1)
    acc = jnp.zeros((tm, cur_ref.shape[1]), F32) + b_ref[...]
    for w in range(CONV_WIDTH):
        s, lo = (base + w) % sub, (base + w) // sub * sub
        window = ext_ref[lo:lo + tm, :] if s == 0 else shift_ref[s, lo:lo + tm, :]
        acc = acc + window * w_ref[w:w + 1, :]
    o_ref[...] = _ln_silu(acc, g_ref[...], be_ref[...]).astype(BF16)


def _conv_prompt(u, w_dw, b_dw, ln_g, ln_b, n_prompt, seq):
    d = u.shape[1]
    tm, halo = CONV_TILE, CONV_HALO
    ratio = tm // halo
    fixed = lambda i: (0, 0)
    vec = lambda a: a.reshape(1, d)
    return pl.pallas_call(
        functools.partial(_conv_prompt_kernel, tiles_per_seq=seq // tm),
        grid=(n_prompt // tm,),
        in_specs=[pl.BlockSpec((halo, d), lambda i: (jnp.maximum(i * ratio - 1, 0), 0)),
                  pl.BlockSpec((tm, d), lambda i: (i, 0)),
                  pl.BlockSpec((CONV_WIDTH, d), fixed),
                  pl.BlockSpec((1, d), fixed), pl.BlockSpec((1, d), fixed), pl.BlockSpec((1, d), fixed)],
        out_specs=pl.BlockSpec((tm, d), lambda i: (i, 0)),
        out_shape=jax.ShapeDtypeStruct((n_prompt, d), BF16),
        scratch_shapes=[pltpu.VMEM((halo + tm, d), F32), pltpu.VMEM((SUBLANES, halo + tm - SUBLANES, d), F32)],
        compiler_params=_cparams("parallel"),
        name="conv_prompt",
    )(u, u, w_dw, vec(b_dw), vec(ln_g), vec(ln_b))


def _conv_sample_kernel(st_ref, u_ref, wst_ref, wu_ref, b_ref, g_ref, be_ref, o_ref):
    st, u = st_ref[...], u_ref[...]
    t_new = u.shape[1]
    for t in range(t_new):
        c = (jnp.sum(st * wst_ref[t][None], axis=1) + jnp.sum(u * wu_ref[t][None], axis=1) + b_ref[...])
        o_ref[t] = _ln_silu(c, g_ref[...], be_ref[...]).astype(BF16)


def _conv_sample(state, u_s, w_dw, b_dw, ln_g, ln_b):
    db, n_left, d = state.shape
    t_new = u_s.shape[1]
    bb = CONV_SAMPLE_BATCH
    r = jnp.arange(n_left)[None, :] - jnp.arange(t_new)[:, None]
    wst = jnp.where((r >= 0)[..., None], w_dw[jnp.clip(r, 0, CONV_WIDTH - 1)], 0.0)
    s = n_left - jnp.arange(t_new)[:, None] + jnp.arange(t_new)[None, :]
    wu = jnp.where((s <= CONV_WIDTH - 1)[..., None], w_dw[jnp.clip(s, 0, CONV_WIDTH - 1)], 0.0)
    fixed2 = lambda i: (0, 0)
    fixed3 = lambda i: (0, 0, 0)
    vec = lambda a: a.reshape(1, d)
    out = pl.pallas_call(
        _conv_sample_kernel,
        grid=(db // bb,),
        in_specs=[pl.BlockSpec((bb, n_left, d), lambda i: (i, 0, 0)),
                  pl.BlockSpec((bb, t_new, d), lambda i: (i, 0, 0)),
                  pl.BlockSpec((t_new, n_left, d), fixed3), pl.BlockSpec((t_new, t_new, d), fixed3),
                  pl.BlockSpec((1, d), fixed2), pl.BlockSpec((1, d), fixed2), pl.BlockSpec((1, d), fixed2)],
        out_specs=pl.BlockSpec((t_new, bb, d), lambda i: (0, i, 0)),
        out_shape=jax.ShapeDtypeStruct((t_new, db, d), BF16),
        compiler_params=_cparams("parallel"),
        name="conv_sample",
    )(state, u_s, wst, wu, vec(b_dw), vec(ln_g), vec(ln_b))
    return out.transpose(1, 0, 2).reshape(db * t_new, d)


def _swa_prompt_kernel(sink_ref, q_ref, kp_ref, kc_ref, vp_ref, vc_ref, o_ref):
    i = pl.program_id(1)
    w = q_ref.shape[0]
    q = q_ref[...]
    k2 = jnp.concatenate([kp_ref[...], kc_ref[...]], axis=0)
    v2 = jnp.concatenate([vp_ref[...], vc_ref[...]], axis=0)
    rows = GROUP * w
    r = lax.broadcasted_iota(jnp.int32, (rows, 2 * w), 0)
    c = lax.broadcasted_iota(jnp.int32, (rows, 2 * w), 1)
    rel = (r & (w - 1)) - c + w
    first_valid = jnp.where(i > 0, 0, w)
    mask = (rel >= 0) & (rel <= WINDOW) & (c >= first_valid)
    outs = []
    for h in range(N_KV_HEADS):
        qh = jnp.concatenate(
            [q[:, (h * GROUP + g) * HEAD_DIM:(h * GROUP + g + 1) * HEAD_DIM] for g in range(GROUP)], axis=0)
        z = _dot_nt(qh, k2[:, h * HEAD_DIM:(h + 1) * HEAD_DIM])
        z = jnp.where(mask, z, -jnp.inf)
        s = jnp.concatenate([jnp.full((w, 1), sink_ref[h * GROUP + g], F32) for g in range(GROUP)], axis=0)
        m = jnp.maximum(jnp.max(z, axis=-1, keepdims=True), s)
        e = jnp.exp(z - m)
        den = jnp.sum(e, axis=-1, keepdims=True) + jnp.exp(s - m)
        o = _dot(e.astype(BF16), v2[:, h * HEAD_DIM:(h + 1) * HEAD_DIM]) / den
        outs += [o[g * w:(g + 1) * w] for g in range(GROUP)]
    o_ref[...] = jnp.concatenate(outs, axis=1).astype(BF16)


def _swa_prompt(sinks, q, k, v, batch, seq):
    w = WINDOW
    nb = seq // w
    cur = lambda b, i: (b * nb + i, 0)
    prev = lambda b, i: (b * nb + jnp.maximum(i - 1, 0), 0)
    return pl.pallas_call(
        _swa_prompt_kernel,
        grid=(batch, nb),
        in_specs=[pl.BlockSpec(memory_space=pltpu.SMEM),
                  pl.BlockSpec((w, Q_DIM), cur),
                  pl.BlockSpec((w, KV_DIM), prev), pl.BlockSpec((w, KV_DIM), cur),
                  pl.BlockSpec((w, KV_DIM), prev), pl.BlockSpec((w, KV_DIM), cur)],
        out_specs=pl.BlockSpec((w, Q_DIM), cur),
        out_shape=jax.ShapeDtypeStruct((batch * seq, Q_DIM), BF16),
        compiler_params=_cparams("parallel", "parallel"),
        name="swa_prompt",
    )(sinks, q, k, k, v, v)


def _swa_sample_kernel(sink_ref, q_ref, kb_ref, vb_ref, kn_ref, vn_ref, o_ref, *, t_new):
    qbd = _block_diag_q(q_ref[0])
    rows = qbd.shape[0]
    wb = kb_ref.shape[1]
    kn = _pad_rows(kn_ref[0], wb).astype(BF16)
    vn = _pad_rows(vn_ref[0], wb).astype(BF16)
    r = lax.broadcasted_iota(jnp.int32, (rows, wb), 0)
    c = lax.broadcasted_iota(jnp.int32, (rows, wb), 1)
    t = _div_pow2(r, GROUP) & (t_new - 1)
    z1 = jnp.where((t + wb - c >= 0) & (t + wb - c <= WINDOW), _dot_nt(qbd, kb_ref[0].astype(BF16)), -jnp.inf)
    z2 = jnp.where((t - c >= 0) & (c < t_new), _dot_nt(qbd, kn), -jnp.inf)
    s = sink_ref[:, 0:1]
    m = jnp.maximum(jnp.maximum(jnp.max(z1, axis=-1, keepdims=True), jnp.max(z2, axis=-1, keepdims=True)), s)
    e1, e2 = jnp.exp(z1 - m), jnp.exp(z2 - m)
    den = jnp.sum(e1, axis=-1, keepdims=True) + jnp.sum(e2, axis=-1, keepdims=True) + jnp.exp(s - m)
    o = (_dot(e1.astype(BF16), vb_ref[0].astype(BF16)) + _dot(e2.astype(BF16), vn)) / den
    o_ref[0] = _diag_blocks(o).astype(BF16)


def _swa_sample(sink_rows, q_rows, k_buf, v_buf, k_new, v_new):
    db, wb, _ = k_buf.shape
    rows = q_rows.shape[1]
    t_pad = k_new.shape[1]
    per_b = lambda b: (b, 0, 0)
    return pl.pallas_call(
        functools.partial(_swa_sample_kernel, t_new=rows // N_Q_HEADS),
        grid=(db,),
        in_specs=[pl.BlockSpec((rows, LANES), lambda b: (0, 0)),
                  pl.BlockSpec((1, rows, HEAD_DIM), per_b),
                  pl.BlockSpec((1, wb, KV_DIM), per_b), pl.BlockSpec((1, wb, KV_DIM), per_b),
                  pl.BlockSpec((1, t_pad, KV_DIM), per_b), pl.BlockSpec((1, t_pad, KV_DIM), per_b)],
        out_specs=pl.BlockSpec((1, rows, HEAD_DIM), per_b),
        out_shape=jax.ShapeDtypeStruct((db, rows, HEAD_DIM), BF16),
        compiler_params=_cparams("parallel"),
        name="swa_sample",
    )(sink_rows, q_rows, k_buf, v_buf, k_new, v_new)


def _sample_q_rows(q_s, db, t):
    return q_s.reshape(db, t, N_KV_HEADS, GROUP, HEAD_DIM).transpose(0, 2, 1, 3, 4).reshape(
        db, N_Q_HEADS * t, HEAD_DIM)


def _sample_o_tokens(o_rows, db, t):
    return o_rows.reshape(db, N_KV_HEADS, t, GROUP, HEAD_DIM).transpose(0, 2, 1, 3, 4).reshape(db * t, Q_DIM)


def _head_rows(per_head, t):
    rows = jnp.broadcast_to(per_head.reshape(N_KV_HEADS, 1, GROUP), (N_KV_HEADS, t, GROUP)).reshape(-1, 1)
    return jnp.broadcast_to(rows, (rows.shape[0], LANES)).astype(F32)


def _pad_new(x, db, t):
    return jnp.pad(x.reshape(db, t, KV_DIM), ((0, 0), (0, 8 - t), (0, 0)))


def _new_keys_on_lanes(x, db, t, width):
    xt = x.reshape(db, t, KV_DIM).transpose(0, 2, 1).astype(BF16)
    return jnp.pad(xt, ((0, 0), (0, 0), (0, width - t)))


def _rope_tables(pos):
    half = ROPE_DIM // 2
    inv = ROPE_THETA ** (-jnp.arange(half, dtype=F32) / half)
    ang = pos.astype(F32)[:, None] * inv[None, :]
    cos, sin = jnp.cos(ang), jnp.sin(ang)
    n = pos.shape[0]
    rest = HEAD_DIM - ROPE_DIM
    c = jnp.concatenate([cos, cos, jnp.ones((n, rest), F32)], axis=1)
    sa = jnp.concatenate([-sin, jnp.zeros((n, half + rest), F32)], axis=1)
    sb = jnp.concatenate([jnp.zeros((n, half), F32), sin, jnp.zeros((n, rest), F32)], axis=1)
    rep = LANES // HEAD_DIM
    return tuple(jnp.concatenate([x] * rep, axis=1) for x in (c, sa, sb))


def kernel(x_prompt, x_sample, cache_sb_k, cache_sb_v, state_conv, cache_swa_k, cache_swa_v, page_table,
           norm_mix, norm_ffn, w_ffn_up, w_ffn_down, w_sb_qkv, w_sb_o, sb_bias,
           w_cv_pw1, b_cv_pw1, w_cv_dw, b_cv_dw, cv_ln_g, cv_ln_b, w_cv_pw2, b_cv_pw2,
           w_swa_qkv, b_swa_qkv, swa_sinks, w_swa_o, b_swa_o, norm_final):
    batch, seq, d = x_prompt.shape
    db, t_new, _ = x_sample.shape
    depth = norm_mix.shape[0]
    past_len = page_table.shape[1] * cache_sb_k.shape[2]
    n_p, n_s = batch * seq, db * t_new
    assert (n_p + n_s) % TOKEN_TILE == 0 and n_p % TOKEN_TILE == 0
    assert cache_swa_k.shape[2] == WINDOW and t_new <= 8

    x = jnp.concatenate([x_prompt.reshape(n_p, d), x_sample.reshape(n_s, d)], axis=0)
    n_phys, page = cache_sb_k.shape[1], cache_sb_k.shape[2]
    cache_kt = cache_sb_k.transpose(0, 1, 3, 4, 2).reshape(cache_sb_k.shape[0], n_phys, KV_DIM, page)
    cache_vt = cache_sb_v.transpose(0, 1, 3, 4, 2).reshape(cache_sb_v.shape[0], n_phys, KV_DIM, page)
    jj = jnp.arange(SB_K_BLOCK)
    lt = (jj[:, None] >= jj[None, :]).astype(BF16)
    zero_bias_d = jnp.zeros((d,), F32)

    outs = {k: [] for k in ("sbkp", "sbvp", "sbks", "sbvs", "cvp", "cvs", "swkp", "swvp", "swks", "swvs")}
    a_i = c_i = w_i = 0
    for layer in range(depth):
        kind = layer % N_MIXERS
        if kind == 0:
            w_qkv = w_sb_qkv[a_i].astype(BF16)
            no_bias = jnp.zeros((Q_DIM + 2 * KV_DIM,), F32)
            bias2 = sb_bias[a_i] * LOG2E
            tiles_per_seq = seq // TOKEN_TILE
            blocks_per_tile = TOKEN_TILE // SB_K_BLOCK
            kvt_f32 = (jax.ShapeDtypeStruct((batch, KV_DIM, seq), F32),
                       pl.BlockSpec((1, KV_DIM, TOKEN_TILE), lambda i: (i // tiles_per_seq, 0, i % tiles_per_seq)))
            kvt_blk = (jax.ShapeDtypeStruct((n_p // SB_K_BLOCK, KV_DIM, SB_K_BLOCK), BF16),
                       pl.BlockSpec((blocks_per_tile, KV_DIM, SB_K_BLOCK), lambda i: (i, 0, 0)))
            q_p, ktf, vtf, ktb, vtb = _norm_linear(
                x, norm_mix[layer], w_qkv, no_bias, _sb_qkv_prompt_epilogue,
                [_row_out(n_p, Q_DIM, BF16), kvt_f32, kvt_f32, kvt_blk, kvt_blk],
                n_tiles=n_p // TOKEN_TILE, name="sb_qkv_prompt")
            q_s, k_s, v_s = _norm_linear(
                x, norm_mix[layer], w_qkv, no_bias, _sb_qkv_epilogue,
                [_row_out(n_s, Q_DIM, BF16), _row_out(n_s, KV_DIM, F32), _row_out(n_s, KV_DIM, F32)],
                first_tile=n_p // TOKEN_TILE, name="sb_qkv_sample")
            o_p = _sb_prompt(bias2, q_p, ktb, vtb, lt, batch, seq)
            o_s = _sb_sample(page_table, _head_rows(bias2, t_new), _sample_q_rows(q_s, db, t_new),
                             _new_keys_on_lanes(k_s, db, t_new, page), _new_keys_on_lanes(v_s, db, t_new, page),
                             lt, cache_kt, cache_vt, a_i)
            o_s = _sample_o_tokens(o_s, db, t_new)
            w_o, b_o = w_sb_o[a_i], zero_bias_d
            heads_t = lambda t: t.reshape(batch, N_KV_HEADS, HEAD_DIM, seq).transpose(0, 3, 1, 2)
            outs["sbkp"].append(heads_t(ktf))
            outs["sbvp"].append(heads_t(vtf))
            outs["sbks"].append(k_s.reshape(db, t_new, N_KV_HEADS, HEAD_DIM))
            outs["sbvs"].append(v_s.reshape(db, t_new, N_KV_HEADS, HEAD_DIM))
            a_i += 1
        elif kind == 1:
            (u,) = _norm_linear(x, norm_mix[layer], w_cv_pw1[c_i].astype(BF16), b_cv_pw1[c_i],
                                _glu_epilogue, [_row_out(n_p + n_s, d, F32)], name="conv_pw1")
            u_s = u[n_p:].reshape(db, t_new, d)
            o_p = _conv_prompt(u, w_cv_dw[c_i], b_cv_dw[c_i], cv_ln_g[c_i], cv_ln_b[c_i], n_p, seq)
            o_s = _conv_sample(state_conv[c_i], u_s, w_cv_dw[c_i], b_cv_dw[c_i], cv_ln_g[c_i], cv_ln_b[c_i])
            w_o, b_o = w_cv_pw2[c_i], b_cv_pw2[c_i]
            n_left = CONV_WIDTH - 1
            outs["cvp"].append(jnp.stack([u[(b + 1) * seq - n_left:(b + 1) * seq] for b in range(batch)]))
            outs["cvs"].append(jnp.concatenate([state_conv[c_i], u_s], axis=1)[:, -n_left:])
            c_i += 1
        else:
            pos = jnp.concatenate([jnp.tile(jnp.arange(seq, dtype=jnp.int32), batch),
                                   jnp.tile(past_len + jnp.arange(t_new, dtype=jnp.int32), db)])
            q, k, v = _norm_linear(
                x, norm_mix[layer], w_swa_qkv[w_i].astype(BF16), b_swa_qkv[w_i],
                _swa_qkv_epilogue,
                [_row_out(n_p + n_s, Q_DIM, BF16), _row_out(n_p + n_s, KV_DIM, F32),
                 _row_out(n_p + n_s, KV_DIM, F32)],
                extra=_rope_tables(pos), name="swa_qkv")
            kp, vp = k[:n_p], v[:n_p]
            o_p = _swa_prompt(swa_sinks[w_i], q, kp.astype(BF16), vp.astype(BF16), batch, seq)
            k_new, v_new = k[n_p:].reshape(db, t_new, KV_DIM), v[n_p:].reshape(db, t_new, KV_DIM)
            k_buf = cache_swa_k[w_i].reshape(db, WINDOW, KV_DIM)
            v_buf = cache_swa_v[w_i].reshape(db, WINDOW, KV_DIM)
            o_s = _swa_sample(_head_rows(swa_sinks[w_i], t_new), _sample_q_rows(q[n_p:], db, t_new),
                              k_buf, v_buf, _pad_new(k[n_p:], db, t_new), _pad_new(v[n_p:], db, t_new))
            o_s = _sample_o_tokens(o_s, db, t_new)
            w_o, b_o = w_swa_o[w_i], b_swa_o[w_i]
            heads = (N_KV_HEADS, HEAD_DIM)
            outs["swkp"].append(kp.reshape(batch, seq, *heads)[:, seq - WINDOW:])
            outs["swvp"].append(vp.reshape(batch, seq, *heads)[:, seq - WINDOW:])
            outs["swks"].append(jnp.concatenate([k_buf, k_new], axis=1)[:, -WINDOW:].reshape(db, WINDOW, *heads))
            outs["swvs"].append(jnp.concatenate([v_buf, v_new], axis=1)[:, -WINDOW:].reshape(db, WINDOW, *heads))
            w_i += 1
        x = _proj_mlp(o_p, o_s, w_o.astype(BF16), b_o, x, norm_ffn[layer], w_ffn_up[layer].astype(BF16),
                      w_ffn_down[layer].astype(BF16), norm_final, final_norm=(layer == depth - 1))

    y_prompt, y_sample = x
    return (y_prompt.reshape(batch, seq, d), y_sample.reshape(db, t_new, d),
            jnp.stack(outs["sbkp"]), jnp.stack(outs["sbvp"]), jnp.stack(outs["sbks"]), jnp.stack(outs["sbvs"]),
            jnp.stack(outs["cvp"]), jnp.stack(outs["cvs"]),
            jnp.stack(outs["swkp"]), jnp.stack(outs["swvp"]), jnp.stack(outs["swks"]), jnp.stack(outs["swvs"]))
```
